```python
import jax
import jax.numpy as jnp
from jax import lax
import numpy as np

D_MODEL = 4096
BATCH = 4
SEQ = 2048
DEPTH = 2
DEC_BATCH = 128
DEC_SEQ = 4
PAST_LEN = 16384
PAGE_SIZE = 128

N_MIXERS = 2
N_ATTN_LAYERS = (DEPTH + 1) // 2
N_POOL_LAYERS = DEPTH // 2
N_HEADS = 32
QK_NOPE = 128
QK_ROPE = 64
V_HEAD = 128
Q_LORA = 1024
KV_LORA = 512
ROPE_THETA = 10000.0
SM_SCALE = (QK_NOPE + QK_ROPE) ** -0.5
Q_BLOCK = 128
POOL_WINDOWS = (2, 4, 8, 16)
POOL_GROUPS = len(POOL_WINDOWS)
POOL_GROUP = D_MODEL // POOL_GROUPS
POOL_MAX = max(POOL_WINDOWS)
N_EXPERTS = 64
TOP_K = 8
N_GROUPS = 8
TOPK_GROUPS = 4
D_EXPERT = 1024
D_SHARED = 1024
ROUTED_SCALE = 2.5
MOE_BLOCK = 128
EPS = 1e-6

kernel_name = 'hybrid_mla_pool_moe_adaln_step'


def rms_norm(x, g):
    x32 = x.astype(jnp.float32)
    y = x32 * lax.rsqrt(jnp.mean(x32 * x32, axis=-1, keepdims=True) + EPS)
    return (y * g.astype(jnp.float32)).astype(x.dtype)


def modulation(c, w, b):
    m = jax.nn.silu(c) @ w + b
    return [t[:, None, :] for t in jnp.split(m, 6, axis=-1)]


def modulated_norm(x, g, shift, scale):
    return rms_norm(x, g) * (1 + scale) + shift


def rope(x, pos):
    half = QK_ROPE // 2
    inv_freq = ROPE_THETA ** (-jnp.arange(half, dtype=jnp.float32) / half)
    ang = pos.astype(jnp.float32)[:, None] * inv_freq[None, :]
    shape = (pos.shape[0],) + (1,) * (x.ndim - 3) + (half,)
    cos = jnp.cos(ang).reshape(shape)
    sin = jnp.sin(ang).reshape(shape)
    x32 = x.astype(jnp.float32)
    x1, x2 = x32[..., :half], x32[..., half:]
    return jnp.concatenate([x1 * cos - x2 * sin, x2 * cos + x1 * sin], axis=-1).astype(x.dtype)


def mla_project(h, pos, w_in, q_norm_g, w_uq, kv_norm_g):
    n, s = h.shape[:2]
    a = h @ w_in
    c_q = rms_norm(a[..., :Q_LORA], q_norm_g)
    c_kv = rms_norm(a[..., Q_LORA:Q_LORA + KV_LORA], kv_norm_g)
    k_rope = rope(a[..., Q_LORA + KV_LORA:], pos)
    q = (c_q @ w_uq).reshape(n, s, N_HEADS, QK_NOPE + QK_ROPE)
    q_nope = q[..., :QK_NOPE]
    q_rope = rope(q[..., QK_NOPE:], pos)
    return q_nope, q_rope, c_kv, k_rope


def mla_prompt_attention(q_nope, q_rope, c_kv, k_rope, w_uk, w_uv):
    n, s = c_kv.shape[:2]
    k_nope = (c_kv @ w_uk).reshape(n, s, N_HEADS, QK_NOPE)
    v = (c_kv @ w_uv).reshape(n, s, N_HEADS, V_HEAD)
    nb = s // Q_BLOCK
    qn = q_nope.reshape(n, nb, Q_BLOCK, N_HEADS, QK_NOPE).swapaxes(0, 1)
    qr = q_rope.reshape(n, nb, Q_BLOCK, N_HEADS, QK_ROPE).swapaxes(0, 1)
    key_pos = jnp.arange(s)

    def block(args):
        qn_b, qr_b, b = args
        sc = (jnp.einsum('bqhd,bkhd->bhqk', qn_b, k_nope)
              + jnp.einsum('bqhr,bkr->bhqk', qr_b, k_rope)).astype(jnp.float32) * SM_SCALE
        q_pos = b * Q_BLOCK + jnp.arange(Q_BLOCK)
        sc = jnp.where(key_pos[None, :] <= q_pos[:, None], sc, -jnp.inf)
        p = jax.nn.softmax(sc, axis=-1).astype(v.dtype)
        return jnp.einsum('bhqk,bkhd->bqhd', p, v)

    o = lax.map(block, (qn, qr, jnp.arange(nb)))
    return o.swapaxes(0, 1).reshape(n, s, N_HEADS * V_HEAD)


def mla_sample_attention(q_nope, q_rope, c_kv, k_rope, cache_lat, cache_rope, page_table, w_uk, w_uv):
    n, l = c_kv.shape[:2]
    past = page_table.shape[1] * cache_lat.shape[1]
    w_uk_h = w_uk.reshape(KV_LORA, N_HEADS, QK_NOPE)
    w_uv_h = w_uv.reshape(KV_LORA, N_HEADS, V_HEAD)
    q_lat = jnp.einsum('nlhd,chd->nlhc', q_nope, w_uk_h)
    new_ok = jnp.arange(l)[None, :] <= jnp.arange(l)[:, None]
    mask = jnp.concatenate([jnp.ones((l, past), bool), new_ok], axis=1)

    def one_seq(args):
        pages, ql, qr, ckv, kr = args
        lat = jnp.concatenate([cache_lat[pages].reshape(past, KV_LORA), ckv], axis=0)
        kro = jnp.concatenate([cache_rope[pages].reshape(past, QK_ROPE), kr], axis=0)
        sc = (jnp.einsum('lhc,kc->hlk', ql, lat)
              + jnp.einsum('lhr,kr->hlk', qr, kro)).astype(jnp.float32) * SM_SCALE
        sc = jnp.where(mask[None], sc, -jnp.inf)
        p = jax.nn.softmax(sc, axis=-1).astype(lat.dtype)
        return jnp.einsum('hlk,kc->lhc', p, lat)

    o_lat = lax.map(one_seq, (page_table, q_lat, q_rope, c_kv, k_rope))
    o = jnp.einsum('nlhc,chd->nlhd', o_lat, w_uv_h)
    return o.reshape(n, l, N_HEADS * V_HEAD)


def causal_window_mean(u, w, pos0):
    t = u.shape[1]
    cs = jnp.cumsum(u, axis=1)
    lagged = jnp.pad(cs, ((0, 0), (w, 0), (0, 0)))[:, :t]
    count = jnp.minimum(pos0 + jnp.arange(t) + 1, w).astype(jnp.float32)
    return (cs - lagged) / count[None, :, None]


def pool_mix(hh, pos0, n_out, pool_w, pool_scale):
    n, t, _ = hh.shape
    u = hh.astype(jnp.float32).reshape(n, t, POOL_GROUPS, POOL_GROUP)
    pooled = jnp.stack([causal_window_mean(u[:, :, g], w, pos0)
                        for g, w in enumerate(POOL_WINDOWS)], axis=2)
    d = (pooled - u)[:, t - n_out:].astype(hh.dtype)
    y = jnp.einsum('nsgc,gcd->nsgd', d, pool_w).reshape(n, n_out, D_MODEL)
    return y * pool_scale


def swiglu(x, w_gate, w_up, w_down):
    return (jax.nn.silu(x @ w_gate) * (x @ w_up)) @ w_down


def route(h, router_w, router_bias):
    t = h.shape[0]
    scores = jax.nn.sigmoid(h.astype(jnp.float32) @ router_w.astype(jnp.float32))
    sel = scores + router_bias.astype(jnp.float32)
    grp = sel.reshape(t, N_GROUPS, N_EXPERTS // N_GROUPS)
    grp_score = lax.top_k(grp, 2)[0].sum(-1)
    _, top_groups = lax.top_k(grp_score, TOPK_GROUPS)
    group_mask = jax.nn.one_hot(top_groups, N_GROUPS).sum(1) > 0
    expert_mask = jnp.repeat(group_mask, N_EXPERTS // N_GROUPS, axis=1)
    _, idx = lax.top_k(jnp.where(expert_mask, sel, -jnp.inf), TOP_K)
    wts = jnp.take_along_axis(scores, idx, axis=1)
    wts = wts / jnp.sum(wts, axis=-1, keepdims=True) * ROUTED_SCALE
    return idx, wts


def routed_experts(h, idx, wts, w_gate, w_up, w_down):
    t, d = h.shape
    e_count = w_gate.shape[0]
    a = t * TOP_K
    flat_e = idx.reshape(a)
    order = jnp.argsort(flat_e)
    e_sorted = flat_e[order]
    counts = jnp.bincount(flat_e, length=e_count)
    padded = (counts + MOE_BLOCK - 1) // MOE_BLOCK * MOE_BLOCK
    start = jnp.cumsum(counts) - counts
    pend = jnp.cumsum(padded)
    pstart = pend - padded
    slot = pstart[e_sorted] + jnp.arange(a) - start[e_sorted]
    n_blocks = -(-a // MOE_BLOCK) + e_count
    n_slots = n_blocks * MOE_BLOCK
    slot_tok = jnp.full((n_slots,), t, jnp.int32).at[slot].set((order // TOP_K).astype(jnp.int32))
    slot_w = jnp.zeros((n_slots,), h.dtype).at[slot].set(wts.reshape(a)[order].astype(h.dtype))
    block_e = jnp.minimum(jnp.searchsorted(pend, jnp.arange(n_blocks) * MOE_BLOCK, side='right'),
                          e_count - 1)
    h_pad = jnp.concatenate([h, jnp.zeros((1, d), h.dtype)], axis=0)

    def block(args):
        toks, e = args
        return swiglu(h_pad[toks], w_gate[e], w_up[e], w_down[e])

    out = lax.map(block, (slot_tok.reshape(n_blocks, MOE_BLOCK), block_e))
    y = jnp.zeros((t + 1, d), h.dtype).at[slot_tok].add(out.reshape(n_slots, d) * slot_w[:, None])
    return y[:t]


def moe_ffn(h, router_w, router_bias, w_gate, w_up, w_down, sw_gate, sw_up, sw_down):
    n, s, d = h.shape
    flat = h.reshape(n * s, d)
    idx, wts = route(flat, router_w, router_bias)
    y = routed_experts(flat, idx, wts, w_gate, w_up, w_down) + swiglu(flat, sw_gate, sw_up, sw_down)
    return y.reshape(n, s, d)


def setup_inputs(seed: int = 0) -> dict:
    key = jax.random.key(seed)
    ks = iter(jax.random.split(key, 40))

    def nrm(shape, std):
        return std * jax.random.normal(next(ks), shape, jnp.float32)

    def gain(shape):
        return 1.0 + 0.02 * jax.random.normal(next(ks), shape, jnp.float32)

    n_pages = PAST_LEN // PAGE_SIZE
    n_phys = (DEC_BATCH * n_pages * 5) // 4
    perm = jax.random.permutation(next(ks), n_phys)
    page_table = perm[:DEC_BATCH * n_pages].reshape(DEC_BATCH, n_pages).astype(jnp.int32)
    d, e, f = D_MODEL, N_EXPERTS, D_EXPERT
    return {
        'x_prompt': nrm((BATCH, SEQ, d), 1.0),
        'x_sample': nrm((DEC_BATCH, DEC_SEQ, d), 1.0),
        'cache_kv_latent': nrm((N_ATTN_LAYERS, n_phys, PAGE_SIZE, KV_LORA), 1.0),
        'cache_k_rope': nrm((N_ATTN_LAYERS, n_phys, PAGE_SIZE, QK_ROPE), 1.0),
        'state_pool': nrm((N_POOL_LAYERS, DEC_BATCH, POOL_MAX - 1, d), 1.0),
        'page_table': page_table,
        'c_prompt': nrm((BATCH, d), 1.0),
        'c_sample': nrm((DEC_BATCH, d), 1.0),
        'ada_w': nrm((DEPTH, d, 6 * d), 0.5 * d ** -0.5),
        'ada_b': nrm((DEPTH, 6 * d), 0.02),
        'norm_mix_g': gain((DEPTH, d)),
        'norm_ffn_g': gain((DEPTH, d)),
        'mla_w_in': nrm((N_ATTN_LAYERS, d, Q_LORA + KV_LORA + QK_ROPE), d ** -0.5),
        'mla_q_norm_g': gain((N_ATTN_LAYERS, Q_LORA)),
        'mla_w_uq': nrm((N_ATTN_LAYERS, Q_LORA, N_HEADS * (QK_NOPE + QK_ROPE)), Q_LORA ** -0.5),
        'mla_kv_norm_g': gain((N_ATTN_LAYERS, KV_LORA)),
        'mla_w_uk': nrm((N_ATTN_LAYERS, KV_LORA, N_HEADS * QK_NOPE), KV_LORA ** -0.5),
        'mla_w_uv': nrm((N_ATTN_LAYERS, KV_LORA, N_HEADS * V_HEAD), KV_LORA ** -0.5),
        'mla_w_o': nrm((N_ATTN_LAYERS, N_HEADS * V_HEAD, d), (N_HEADS * V_HEAD) ** -0.5),
        'pool_w': nrm((N_POOL_LAYERS, POOL_GROUPS, POOL_GROUP, POOL_GROUP), POOL_GROUP ** -0.5),
        'pool_scale': 1.0 + nrm((N_POOL_LAYERS, d), 0.1),
        'router_w': nrm((DEPTH, d, e), d ** -0.5),
        'router_bias': nrm((DEPTH, e), 0.01),
        'exp_w_gate': nrm((DEPTH, e, d, f), d ** -0.5),
        'exp_w_up': nrm((DEPTH, e, d, f), d ** -0.5),
        'exp_w_down': nrm((DEPTH, e, f, d), f ** -0.5),
        'shared_w_gate': nrm((DEPTH, d, D_SHARED), d ** -0.5),
        'shared_w_up': nrm((DEPTH, d, D_SHARED), d ** -0.5),
        'shared_w_down': nrm((DEPTH, D_SHARED, d), D_SHARED ** -0.5),
        'final_norm_g': gain((d,)),
    }


def reference(x_prompt, x_sample, cache_kv_latent, cache_k_rope, state_pool, page_table, c_prompt,
              c_sample, ada_w, ada_b, norm_mix_g, norm_ffn_g, mla_w_in, mla_q_norm_g, mla_w_uq,
              mla_kv_norm_g, mla_w_uk, mla_w_uv, mla_w_o, pool_w, pool_scale, router_w, router_bias,
              exp_w_gate, exp_w_up, exp_w_down, shared_w_gate, shared_w_up, shared_w_down,
              final_norm_g):
    seq = x_prompt.shape[1]
    dec_seq = x_sample.shape[1]
    past = page_table.shape[1] * cache_kv_latent.shape[2]
    pos_p = jnp.arange(seq)
    pos_s = past + jnp.arange(dec_seq)
    xp, xs = x_prompt, x_sample
    lat_p, rope_p, pool_p, lat_s, rope_s, pool_s = [], [], [], [], [], []
    for i in range(DEPTH):
        j = i // N_MIXERS
        sh_mp, sc_mp, g_mp, sh_fp, sc_fp, g_fp = modulation(c_prompt, ada_w[i], ada_b[i])
        sh_ms, sc_ms, g_ms, sh_fs, sc_fs, g_fs = modulation(c_sample, ada_w[i], ada_b[i])
        hp = modulated_norm(xp, norm_mix_g[i], sh_mp, sc_mp)
        hs = modulated_norm(xs, norm_mix_g[i], sh_ms, sc_ms)
        if i % N_MIXERS == 0:
            qn, qr, ckv, kr = mla_project(hp, pos_p, mla_w_in[j], mla_q_norm_g[j], mla_w_uq[j],
                                          mla_kv_norm_g[j])
            mp = mla_prompt_attention(qn, qr, ckv, kr, mla_w_uk[j], mla_w_uv[j]) @ mla_w_o[j]
            lat_p.append(ckv)
            rope_p.append(kr)
            qn, qr, ckv, kr = mla_project(hs, pos_s, mla_w_in[j], mla_q_norm_g[j], mla_w_uq[j],
                                          mla_kv_norm_g[j])
            ms = mla_sample_attention(qn, qr, ckv, kr, cache_kv_latent[j], cache_k_rope[j],
                                      page_table, mla_w_uk[j], mla_w_uv[j]) @ mla_w_o[j]
            lat_s.append(ckv)
            rope_s.append(kr)
        else:
            mp = pool_mix(hp, 0, seq, pool_w[j], pool_scale[j])
            pool_p.append(hp[:, seq - (POOL_MAX - 1):])
            hh = jnp.concatenate([state_pool[j], hs], axis=1)
            ms = pool_mix(hh, past - (POOL_MAX - 1), dec_seq, pool_w[j], pool_scale[j])
            pool_s.append(hh[:, hh.shape[1] - (POOL_MAX - 1):])
        xp = xp + g_mp * mp
        xs = xs + g_ms * ms
        ffn_w = (router_w[i], router_bias[i], exp_w_gate[i], exp_w_up[i], exp_w_down[i],
                 shared_w_gate[i], shared_w_up[i], shared_w_down[i])
        xp = xp + g_fp * moe_ffn(modulated_norm(xp, norm_ffn_g[i], sh_fp, sc_fp), *ffn_w)
        xs = xs + g_fs * moe_ffn(modulated_norm(xs, norm_ffn_g[i], sh_fs, sc_fs), *ffn_w)
    y_prompt = rms_norm(xp, final_norm_g)
    y_sample = rms_norm(xs, final_norm_g)
    return (y_prompt, y_sample, jnp.stack(lat_p), jnp.stack(rope_p), jnp.stack(pool_p),
            jnp.stack(lat_s), jnp.stack(rope_s), jnp.stack(pool_s))
```

```python
import functools

import jax
import jax.numpy as jnp
from jax import lax
from jax.experimental import pallas as pl
from jax.experimental.pallas import tpu as pltpu

F32 = jnp.float32
BF16 = jnp.bfloat16
U32 = jnp.uint32
I32 = jnp.int32

EPS = 1e-6
ROPE_THETA = 10000.0
ROUTED_SCALE = 2.5
N_GROUPS = 8
TOPK_GROUPS = 4
TOP_K = 8
POOL_WINDOWS = (2, 4, 8, 16)
POOL_HALO = 16
N_HEADS = 32
QK_NOPE = 128
QK_ROPE = 64
V_HEAD = 128
LANES = 128
HEAD_PAD = 2 * LANES
VMEM_LIMIT = 56 * 1024 * 1024
NEG_BIG = -1e30


def _cp(*sem, vmem=VMEM_LIMIT):
    return pltpu.CompilerParams(dimension_semantics=sem, vmem_limit_bytes=vmem)


def _dot(a, b):
    return jnp.dot(a, b, preferred_element_type=F32)


def _dot_nt(a, b):
    return lax.dot_general(a, b, (((1,), (1,)), ((), ())), preferred_element_type=F32)


def _rms(x, g):
    return x * lax.rsqrt(jnp.mean(x * x, axis=-1, keepdims=True) + EPS) * g


def _silu(x):
    return x * jax.nn.sigmoid(x)


def _unpack_lo(p):
    return lax.bitcast_convert_type(p << 16, F32)


def _unpack_hi(p):
    return lax.bitcast_convert_type(p & jnp.uint32(0xFFFF0000), F32)


def _pack_words(lo, hi):
    lo_b = lax.bitcast_convert_type(lo.astype(BF16).astype(F32), U32)
    hi_b = lax.bitcast_convert_type(hi.astype(BF16).astype(F32), U32)
    return (hi_b & jnp.uint32(0xFFFF0000)) | (lo_b >> 16)


def _adaln_kernel(c_ref, w_ref, b_ref, o_ref):
    s = _silu(c_ref[...]).astype(BF16)
    o_ref[0] = _dot(s, w_ref[0].astype(BF16)) + b_ref[0]


def adaln(c_all, ada_w, ada_b, tn=512):
    depth, d, n = ada_w.shape
    r = c_all.shape[0]
    return pl.pallas_call(
        _adaln_kernel,
        grid=(depth, n // tn),
        in_specs=[pl.BlockSpec((r, d), lambda l, j: (0, 0)),
                  pl.BlockSpec((1, d, tn), lambda l, j: (l, 0, j)),
                  pl.BlockSpec((1, 1, tn), lambda l, j: (l, 0, j))],
        out_specs=pl.BlockSpec((1, r, tn), lambda l, j: (l, 0, j)),
        out_shape=jax.ShapeDtypeStruct((depth, r, n), F32),
        compiler_params=_cp("arbitrary", "arbitrary"),
        name="adaln",
    )(c_all, ada_w, ada_b.reshape(depth, 1, n))


class Mods:
    def __init__(self, mp, ms, seq, n_prompt_tok, d):
        self.mp, self.ms, self.seq, self.tp, self.d = mp, ms, seq, n_prompt_tok, d

    def specs(self, k, tm, row_of):
        npb = self.tp // tm
        per_seq = self.seq // tm
        n_p = self.mp.shape[0]

        def p_map(*g):
            return (jnp.minimum(row_of(*g) // per_seq, n_p - 1), 0, k)

        def s_map(*g):
            return (jnp.maximum(row_of(*g) - npb, 0), k)

        return [pl.BlockSpec((1, 1, self.d), p_map), pl.BlockSpec((tm, self.d), s_map)]

    def args(self):
        return [self.mp, self.ms]


def _pick(is_sample, p_ref, s_ref):
    return jnp.where(is_sample, s_ref[...], p_ref[0])


def _modnorm_kernel(npb, x_ref, g_ref, shp_ref, shs_ref, scp_ref, scs_ref, o_ref):
    smp = pl.program_id(0) >= npb
    shift = _pick(smp, shp_ref, shs_ref)
    scale = _pick(smp, scp_ref, scs_ref)
    o_ref[...] = (_rms(x_ref[...], g_ref[...]) * (1.0 + scale) + shift).astype(o_ref.dtype)


def modnorm(x, g, mods, k_shift, k_scale, out_dtype, tm=256):
    t, d = x.shape
    row = lambda i: i
    return pl.pallas_call(
        functools.partial(_modnorm_kernel, mods.tp // tm),
        grid=(t // tm,),
        in_specs=[pl.BlockSpec((tm, d), lambda i: (i, 0)),
                  pl.BlockSpec((1, d), lambda i: (0, 0)),
                  *mods.specs(k_shift, tm, row), *mods.specs(k_scale, tm, row)],
        out_specs=pl.BlockSpec((tm, d), lambda i: (i, 0)),
        out_shape=jax.ShapeDtypeStruct((t, d), out_dtype),
        compiler_params=_cp("arbitrary"),
        name="modnorm",
    )(x, g.reshape(1, d), *mods.args(), *mods.args())


def _rmsnorm_kernel(x_ref, g_ref, o_ref):
    o_ref[...] = _rms(x_ref[...], g_ref[...])


def rmsnorm(x, g, tm=256):
    t, d = x.shape
    return pl.pallas_call(
        _rmsnorm_kernel,
        grid=(t // tm,),
        in_specs=[pl.BlockSpec((tm, d), lambda i: (i, 0)), pl.BlockSpec((1, d), lambda i: (0, 0))],
        out_specs=pl.BlockSpec((tm, d), lambda i: (i, 0)),
        out_shape=jax.ShapeDtypeStruct((t, d), F32),
        compiler_params=_cp("arbitrary"),
        name="final_rmsnorm",
    )(x, g.reshape(1, d))


def _mla_in_kernel(npb, q_lora, kv_lora, x_ref, g_ref, shp_ref, shs_ref, scp_ref, scs_ref,
                   w_ref, gq_ref, gkv_ref, cos_ref, sin_ref,
                   cq_ref, ckv_ref, ckvb_ref, kr_ref, krb_ref):
    smp = pl.program_id(0) >= npb
    shift = _pick(smp, shp_ref, shs_ref)
    scale = _pick(smp, scp_ref, scs_ref)
    h = (_rms(x_ref[...], g_ref[...]) * (1.0 + scale) + shift).astype(BF16)
    a = _dot(h, w_ref[...])
    cq_ref[...] = _rms(a[:, :q_lora], gq_ref[...]).astype(BF16)
    ckv = _rms(a[:, q_lora:q_lora + kv_lora], gkv_ref[...])
    ckv_ref[...] = ckv
    ckvb_ref[...] = ckv.astype(BF16)
    o = q_lora + kv_lora
    kr = a[:, o:o + LANES] * cos_ref[...] + a[:, o + LANES:o + 2 * LANES] * sin_ref[...]
    kr_ref[...] = kr
    krb_ref[...] = kr.astype(BF16)


def mla_in(x, g, mods, w_aug, gq, gkv, cos_t, sin_t, q_lora, kv_lora, tm=256):
    t, d = x.shape
    n_aug = w_aug.shape[1]
    row = lambda i: i
    blk = lambda w: pl.BlockSpec((tm, w), lambda i: (i, 0))
    return pl.pallas_call(
        functools.partial(_mla_in_kernel, mods.tp // tm, q_lora, kv_lora),
        grid=(t // tm,),
        in_specs=[blk(d), pl.BlockSpec((1, d), lambda i: (0, 0)),
                  *mods.specs(0, tm, row), *mods.specs(1, tm, row),
                  pl.BlockSpec((d, n_aug), lambda i: (0, 0)),
                  pl.BlockSpec((1, q_lora), lambda i: (0, 0)),
                  pl.BlockSpec((1, kv_lora), lambda i: (0, 0)),
                  blk(LANES), blk(LANES)],
        out_specs=[blk(q_lora), blk(kv_lora), blk(kv_lora), blk(LANES), blk(LANES)],
        out_shape=[jax.ShapeDtypeStruct((t, q_lora), BF16),
                   jax.ShapeDtypeStruct((t, kv_lora), F32),
                   jax.ShapeDtypeStruct((t, kv_lora), BF16),
                   jax.ShapeDtypeStruct((t, LANES), F32),
                   jax.ShapeDtypeStruct((t, LANES), BF16)],
        compiler_params=_cp("arbitrary"),
        name="mla_in",
    )(x, g.reshape(1, d), *mods.args(), *mods.args(), w_aug, gq.reshape(1, -1),
      gkv.reshape(1, -1), cos_t, sin_t)


def _q_up_kernel(hg, cq_ref, w1_ref, w2_ref, cos_ref, sin_ref, o_ref):
    cq = cq_ref[...]
    a = _dot(cq, w1_ref[...])
    r = _dot(cq, w2_ref[...])
    c = cos_ref[...]
    s = sin_ref[...]
    for h in range(hg):
        b = h * HEAD_PAD
        o_ref[:, b:b + LANES] = a[:, b:b + LANES].astype(BF16)
        o_ref[:, b + LANES:b + HEAD_PAD] = (
            a[:, b + LANES:b + HEAD_PAD] * c + r[:, h * LANES:(h + 1) * LANES] * s).astype(BF16)


def q_up(cq, w1, w2, cos_t, sin_t, tm=512, hg=4):
    t, k = cq.shape
    n_h = w1.shape[1] // HEAD_PAD
    return pl.pallas_call(
        functools.partial(_q_up_kernel, hg),
        grid=(n_h // hg, t // tm),
        in_specs=[pl.BlockSpec((tm, k), lambda j, i: (i, 0)),
                  pl.BlockSpec((k, hg * HEAD_PAD), lambda j, i: (0, j)),
                  pl.BlockSpec((k, hg * LANES), lambda j, i: (0, j)),
                  pl.BlockSpec((tm, LANES), lambda j, i: (i, 0)),
                  pl.BlockSpec((tm, LANES), lambda j, i: (i, 0))],
        out_specs=pl.BlockSpec((tm, hg * HEAD_PAD), lambda j, i: (i, j)),
        out_shape=jax.ShapeDtypeStruct((t, n_h * HEAD_PAD), BF16),
        compiler_params=_cp("arbitrary", "arbitrary"),
        name="q_up",
    )(cq, w1, w2, cos_t, sin_t)


def _mm_kernel(a_ref, b_ref, o_ref):
    o_ref[...] = _dot(a_ref[...], b_ref[...]).astype(o_ref.dtype)


def matmul(a, b, out_dtype, tm=512, tn=1024):
    m, k = a.shape
    n = b.shape[1]
    tm, tn = min(tm, m), min(tn, n)
    return pl.pallas_call(
        _mm_kernel,
        grid=(n // tn, m // tm),
        in_specs=[pl.BlockSpec((tm, k), lambda j, i: (i, 0)),
                  pl.BlockSpec((k, tn), lambda j, i: (0, j))],
        out_specs=pl.BlockSpec((tm, tn), lambda j, i: (i, j)),
        out_shape=jax.ShapeDtypeStruct((m, n), out_dtype),
        compiler_params=_cp("arbitrary", "arbitrary"),
        name="matmul",
    )(a, b)


def _proj_res_kernel(npb, a_ref, w_ref, x_ref, gp_ref, gs_ref, o_ref):
    smp = pl.program_id(1) >= npb
    gate = _pick(smp, gp_ref, gs_ref)
    o_ref[...] = x_ref[...] + gate * _dot(a_ref[...], w_ref[...])


def proj_residual(a, w, x, mods, k_gate, tm=256, tn=512):
    t, k = a.shape
    d = w.shape[1]
    npb = mods.tp // tm
    per_seq = mods.seq // tm
    n_p = mods.mp.shape[0]
    cb = mods.d // tn
    gp = pl.BlockSpec((1, 1, tn), lambda j, i: (jnp.minimum(i // per_seq, n_p - 1), 0, k_gate * cb + j))
    gs = pl.BlockSpec((tm, tn), lambda j, i: (jnp.maximum(i - npb, 0), k_gate * cb + j))
    return pl.pallas_call(
        functools.partial(_proj_res_kernel, npb),
        grid=(d // tn, t // tm),
        in_specs=[pl.BlockSpec((tm, k), lambda j, i: (i, 0)),
                  pl.BlockSpec((k, tn), lambda j, i: (0, j)),
                  pl.BlockSpec((tm, tn), lambda j, i: (i, j)), gp, gs],
        out_specs=pl.BlockSpec((tm, tn), lambda j, i: (i, j)),
        out_shape=jax.ShapeDtypeStruct((t, d), F32),
        compiler_params=_cp("arbitrary", "arbitrary"),
        name="proj_residual",
    )(a, w, x, *mods.args())


def _flash_kernel(tq, tk, scale, q_ref, kn_ref, kr_ref, v_ref, o_ref, m_s, l_s, acc_s):
    qi = pl.program_id(2)
    ki = pl.program_id(3)
    nk = pl.num_programs(3)

    @pl.when(ki == 0)
    def _():
        m_s[...] = jnp.full_like(m_s, NEG_BIG)
        l_s[...] = jnp.zeros_like(l_s)
        acc_s[...] = jnp.zeros_like(acc_s)

    @pl.when(ki * tk <= qi * tq + (tq - 1))
    def _():
        k = jnp.concatenate([kn_ref[...], kr_ref[...]], axis=1)
        s = _dot_nt(q_ref[...], k) * scale
        qpos = qi * tq + lax.broadcasted_iota(I32, (tq, tk), 0)
        kpos = ki * tk + lax.broadcasted_iota(I32, (tq, tk), 1)
        s = jnp.where(kpos <= qpos, s, NEG_BIG)
        m_old = m_s[...]
        m_new = jnp.maximum(m_old, jnp.max(s, axis=-1, keepdims=True))
        alpha = jnp.exp(m_old - m_new)
        p = jnp.exp(s - m_new)
        l_s[...] = alpha * l_s[...] + jnp.sum(p, axis=-1, keepdims=True)
        acc_s[...] = alpha * acc_s[...] + _dot(p.astype(BF16), v_ref[...])
        m_s[...] = m_new

    @pl.when(ki == nk - 1)
    def _():
        o_ref[...] = (acc_s[...] / l_s[...]).astype(o_ref.dtype)


def prompt_attention(q_cat, kv, kr_b, n_seq, seq, scale, tq=1024, tk=1024):
    n_h = q_cat.shape[1] // HEAD_PAD
    tq, tk = min(tq, seq), min(tk, seq)
    nq, nk = seq // tq, seq // tk

    def kmap(b, h, qi, ki):
        return jnp.minimum(ki, (qi * tq + tq - 1) // tk)

    return pl.pallas_call(
        functools.partial(_flash_kernel, tq, tk, scale),
        grid=(n_seq, n_h, nq, nk),
        in_specs=[pl.BlockSpec((tq, HEAD_PAD), lambda b, h, qi, ki: (b * nq + qi, h)),
                  pl.BlockSpec((tk, QK_NOPE), lambda b, h, qi, ki: (b * nk + kmap(b, h, qi, ki), h)),
                  pl.BlockSpec((tk, LANES), lambda b, h, qi, ki: (b * nk + kmap(b, h, qi, ki), 0)),
                  pl.BlockSpec((tk, V_HEAD), lambda b, h, qi, ki: (b * nk + kmap(b, h, qi, ki), n_h + h))],
        out_specs=pl.BlockSpec((tq, V_HEAD), lambda b, h, qi, ki: (b * nq + qi, h)),
        out_shape=jax.ShapeDtypeStruct((n_seq * seq, n_h * V_HEAD), BF16),
        scratch_shapes=[pltpu.VMEM((tq, 1), F32), pltpu.VMEM((tq, 1), F32),
                        pltpu.VMEM((tq, V_HEAD), F32)],
        compiler_params=_cp("arbitrary", "arbitrary", "arbitrary", "arbitrary"),
        name="prompt_attention",
    )(q_cat, kv, kr_b, kv)


def _bmm_kernel(a_ref, b_ref, o_ref):
    a = a_ref[...]
    a = a.reshape(a.shape[-2:])
    o = _dot(a, b_ref[0]).astype(o_ref.dtype)
    o_ref[...] = o.reshape(o_ref.shape)


def heads_matmul(a, b, a_spec, out_spec, out_shape):
    n_h = b.shape[0]
    return pl.pallas_call(
        _bmm_kernel,
        grid=(n_h,),
        in_specs=[a_spec, pl.BlockSpec((1,) + b.shape[1:], lambda h: (h, 0, 0))],
        out_specs=out_spec,
        out_shape=out_shape,
        compiler_params=_cp("arbitrary"),
        name="heads_matmul",
    )(a, b)


def _decode_kernel(layer, n_pg, n_grp, dec_seq, scale, pt_ref, ql_ref, qr_ref, lat_hbm, rope_hbm,
                   nlat_ref, nrope_ref, o_ref, lat_buf, rope_buf, sem, m_s, l_s, acc_s):
    i = pl.program_id(0)
    g = pl.program_id(1)
    n_seq = pl.num_programs(0)
    page = lat_hbm.shape[2]

    def copies(seq, grp, slot):
        out = []
        for k in range(n_pg):
            pg = pt_ref[seq, grp * n_pg + k]
            out.append(pltpu.make_async_copy(lat_hbm.at[layer, pg],
                                             lat_buf.at[slot, pl.ds(k * page, page)], sem.at[slot]))
            out.append(pltpu.make_async_copy(rope_hbm.at[layer, pg],
                                             rope_buf.at[slot, pl.ds(k * page, page)], sem.at[slot]))
        return out

    step = i * n_grp + g
    slot = step % 2

    @pl.when(jnp.logical_and(i == 0, g == 0))
    def _():
        for c in copies(0, 0, 0):
            c.start()

    @pl.when(jnp.logical_and(g < n_grp, step + 1 < n_seq * n_grp))
    def _():
        nxt = step + 1
        for c in copies(nxt // n_grp, nxt % n_grp, nxt % 2):
            c.start()

    @pl.when(g == 0)
    def _():
        m_s[...] = jnp.full_like(m_s, NEG_BIG)
        l_s[...] = jnp.zeros_like(l_s)
        acc_s[...] = jnp.zeros_like(acc_s)

    def update(lat, rope, mask):
        s = (_dot_nt(ql_ref[0], lat) + _dot_nt(qr_ref[0], rope)) * scale
        if mask is not None:
            s = jnp.where(mask, s, NEG_BIG)
        m_old = m_s[...]
        m_new = jnp.maximum(m_old, jnp.max(s, axis=-1, keepdims=True))
        alpha = jnp.exp(m_old - m_new)
        p = jnp.exp(s - m_new)
        l_s[...] = alpha * l_s[...] + jnp.sum(p, axis=-1, keepdims=True)
        acc_s[...] = alpha * acc_s[...] + _dot(p.astype(BF16), lat)
        m_s[...] = m_new

    @pl.when(g < n_grp)
    def _():
        for c in copies(i, g, slot):
            c.wait()
        update(lat_buf[slot].astype(BF16), rope_buf[slot].astype(BF16), None)

    @pl.when(g == n_grp)
    def _():
        lat = nlat_ref[0]
        rows, keys = ql_ref.shape[1], lat.shape[0]
        q_tok = lax.broadcasted_iota(I32, (rows, keys), 0) // (rows // dec_seq)
        k_tok = lax.broadcasted_iota(I32, (rows, keys), 1)
        update(lat, nrope_ref[0], k_tok <= q_tok)
        o_ref[0] = (acc_s[...] / l_s[...]).astype(o_ref.dtype)


def sample_attention(q_lat, q_rope, cache_lat, cache_rope, layer, page_table, new_lat, new_rope,
                     dec_seq, scale, n_pg=16):
    n, r, c = q_lat.shape
    page = cache_lat.shape[2]
    rd = cache_rope.shape[3]
    n_pages = page_table.shape[1]
    n_pg = min(n_pg, n_pages)
    n_grp = n_pages // n_pg
    kn = new_lat.shape[1]

    seq_map = lambda i, g, pt: (i, 0, 0)
    in_specs = [pl.BlockSpec((1, r, c), seq_map), pl.BlockSpec((1, r, rd), seq_map),
                pl.BlockSpec(memory_space=pl.ANY), pl.BlockSpec(memory_space=pl.ANY),
                pl.BlockSpec((1, kn, c), seq_map), pl.BlockSpec((1, kn, rd), seq_map)]
    grid_spec = pltpu.PrefetchScalarGridSpec(
        num_scalar_prefetch=1,
        grid=(n, n_grp + 1),
        in_specs=in_specs,
        out_specs=pl.BlockSpec((1, r, c), seq_map),
        scratch_shapes=[pltpu.VMEM((2, n_pg * page, c), F32), pltpu.VMEM((2, n_pg * page, rd), F32),
                        pltpu.SemaphoreType.DMA((2,)),
                        pltpu.VMEM((r, 1), F32), pltpu.VMEM((r, 1), F32), pltpu.VMEM((r, c), F32)])
    return pl.pallas_call(
        functools.partial(_decode_kernel, layer, n_pg, n_grp, dec_seq, scale),
        grid_spec=grid_spec,
        out_shape=jax.ShapeDtypeStruct((n, r, c), BF16),
        compiler_params=_cp("arbitrary", "arbitrary"),
        name="sample_attention",
    )(page_table, q_lat, q_rope, cache_lat, cache_rope, new_lat, new_rope)


def _pool_kernel(ts, grp_w, first_pos, period, h_ref, halo_ref, x_ref, gate_ref, w_ref, ps_ref,
                 o_ref):
    i = pl.program_id(1)
    keep = jnp.where(i > 0, 1.0, 0.0).astype(F32)
    u_all = jnp.concatenate([halo_ref[0] * keep, h_ref[0]], axis=0)
    rows = POOL_HALO + ts
    r = i * ts - POOL_HALO + lax.broadcasted_iota(I32, (rows, 1), 0)
    if period:
        r = (r + period) % period
    pos = first_pos + r
    for g, win in enumerate(POOL_WINDOWS):
        u = u_all[:, g * grp_w:(g + 1) * grp_w]
        s = u
        k = 1
        while k < win:
            s = s + pltpu.roll(s, k, axis=0)
            k *= 2
        count = jnp.clip(pos + 1, 1, win).astype(F32)
        dlt = (s / count - u)[POOL_HALO:].astype(BF16)
        y = _dot(dlt, w_ref[g]) * ps_ref[:, g * grp_w:(g + 1) * grp_w]
        o_ref[0, :, g * grp_w:(g + 1) * grp_w] = (
            x_ref[0, :, g * grp_w:(g + 1) * grp_w] + gate_ref[0, :, g * grp_w:(g + 1) * grp_w] * y)


def pool_mixer(h, x, gate, pool_w, pool_scale, first_pos, ts, period=0):
    n, s, d = h.shape
    grp_w = d // len(POOL_WINDOWS)
    hb = ts // POOL_HALO
    if gate.shape[1] == 1:
        gate_spec = pl.BlockSpec((1, 1, d), lambda b, i: (b, 0, 0))
    else:
        gate_spec = pl.BlockSpec((1, ts, d), lambda b, i: (b, i, 0))
    return pl.pallas_call(
        functools.partial(_pool_kernel, ts, grp_w, first_pos, period),
        grid=(n, s // ts),
        in_specs=[pl.BlockSpec((1, ts, d), lambda b, i: (b, i, 0)),
                  pl.BlockSpec((1, POOL_HALO, d), lambda b, i: (b, jnp.maximum(i * hb - 1, 0), 0)),
                  pl.BlockSpec((1, ts, d), lambda b, i: (b, i, 0)),
                  gate_spec,
                  pl.BlockSpec(pool_w.shape, lambda b, i: (0, 0, 0)),
                  pl.BlockSpec((1, d), lambda b, i: (0, 0))],
        out_specs=pl.BlockSpec((1, ts, d), lambda b, i: (b, i, 0)),
        out_shape=jax.ShapeDtypeStruct((n, s, d), F32),
        compiler_params=_cp("arbitrary", "arbitrary"),
        name="pool_mixer",
    )(h, h, x, gate, pool_w, pool_scale.reshape(1, d))


def _ffn_in_kernel(npb, n_exp, x_ref, g_ref, shp_ref, shs_ref, scp_ref, scs_ref, rw_ref, rb_ref,
                   h_ref, idx_ref, wts_ref):
    smp = pl.program_id(0) >= npb
    shift = _pick(smp, shp_ref, shs_ref)
    scale = _pick(smp, scp_ref, scs_ref)
    h = _rms(x_ref[...], g_ref[...]) * (1.0 + scale) + shift
    h_ref[...] = h.astype(BF16)
    logits = lax.dot_general(rw_ref[...], h, (((1,), (1,)), ((), ())),
                             preferred_element_type=F32, precision=lax.Precision.HIGHEST)
    scores = jax.nn.sigmoid(logits)
    sel = scores + rb_ref[...]
    tm = sel.shape[1]
    gsz = n_exp // N_GROUPS
    eid = lax.broadcasted_iota(I32, (n_exp, tm), 0)
    gid_of_e = eid // gsz
    gs = []
    for g in range(N_GROUPS):
        blk = sel[g * gsz:(g + 1) * gsz]
        rid = lax.broadcasted_iota(I32, (gsz, tm), 0)
        m1 = jnp.max(blk, axis=0, keepdims=True)
        a1 = jnp.min(jnp.where(blk == m1, rid, gsz), axis=0, keepdims=True)
        m2 = jnp.max(jnp.where(rid == a1, -jnp.inf, blk), axis=0, keepdims=True)
        gs.append(m1 + m2)
    gs = jnp.concatenate(gs, axis=0)
    grow = lax.broadcasted_iota(I32, (N_GROUPS, tm), 0)
    gm_f = jnp.zeros((N_GROUPS, tm), F32)
    work = gs
    for _ in range(TOPK_GROUPS):
        m = jnp.max(work, axis=0, keepdims=True)
        a = jnp.min(jnp.where(work == m, grow, N_GROUPS), axis=0, keepdims=True)
        hit = grow == a
        gm_f = jnp.where(hit, 1.0, gm_f)
        work = jnp.where(hit, -jnp.inf, work)
    emask = jnp.zeros((n_exp, tm), F32)
    for g in range(N_GROUPS):
        emask = jnp.where(gid_of_e == g, gm_f[g:g + 1], emask)
    work = jnp.where(emask > 0.5, sel, -jnp.inf)
    ids, ws = [], []
    for _ in range(TOP_K):
        m = jnp.max(work, axis=0, keepdims=True)
        a = jnp.min(jnp.where(work == m, eid, n_exp), axis=0, keepdims=True)
        hit = eid == a
        ids.append(a)
        ws.append(jnp.sum(jnp.where(hit, scores, 0.0), axis=0, keepdims=True))
        work = jnp.where(hit, -jnp.inf, work)
    ids = jnp.concatenate(ids, axis=0)
    ws = jnp.concatenate(ws, axis=0)
    idx_ref[...] = ids
    wts_ref[...] = ws / jnp.sum(ws, axis=0, keepdims=True) * ROUTED_SCALE


def ffn_in(x, g, mods, router_wt, router_bias, tm=256):
    t, d = x.shape
    n_exp = router_wt.shape[0]
    row = lambda i: i
    return pl.pallas_call(
        functools.partial(_ffn_in_kernel, mods.tp // tm, n_exp),
        grid=(t // tm,),
        in_specs=[pl.BlockSpec((tm, d), lambda i: (i, 0)), pl.BlockSpec((1, d), lambda i: (0, 0)),
                  *mods.specs(3, tm, row), *mods.specs(4, tm, row),
                  pl.BlockSpec((n_exp, d), lambda i: (0, 0)),
                  pl.BlockSpec((n_exp, 1), lambda i: (0, 0))],
        out_specs=[pl.BlockSpec((tm, d), lambda i: (i, 0)),
                   pl.BlockSpec((TOP_K, tm), lambda i: (0, i)),
                   pl.BlockSpec((TOP_K, tm), lambda i: (0, i))],
        out_shape=[jax.ShapeDtypeStruct((t, d), BF16),
                   jax.ShapeDtypeStruct((TOP_K, t), I32),
                   jax.ShapeDtypeStruct((TOP_K, t), F32)],
        compiler_params=_cp("arbitrary"),
        name="ffn_in",
    )(x, g.reshape(1, d), *mods.args(), *mods.args(), router_wt, router_bias.reshape(n_exp, 1))


def _gather_kernel(rows_per_step, idx_ref, src_ref, o_ref, sem):
    base = pl.program_id(0) * rows_per_step

    def copy(r):
        return pltpu.make_async_copy(src_ref.at[pl.ds(idx_ref[base + r], 1)],
                                     o_ref.at[pl.ds(base + r, 1)], sem)

    def issue(r, c):
        copy(r).start()
        return c

    lax.fori_loop(0, rows_per_step, issue, 0)

    def drain(r, c):
        copy(r).wait()
        return c

    lax.fori_loop(0, rows_per_step, drain, 0)


def gather_rows(src, idx, rows_per_step=512):
    n = idx.shape[0]
    w = src.shape[1]
    grid_spec = pltpu.PrefetchScalarGridSpec(
        num_scalar_prefetch=1,
        grid=(n // rows_per_step,),
        in_specs=[pl.BlockSpec(memory_space=pl.ANY)],
        out_specs=pl.BlockSpec(memory_space=pl.ANY),
        scratch_shapes=[pltpu.SemaphoreType.DMA(())])
    return pl.pallas_call(
        functools.partial(_gather_kernel, rows_per_step),
        grid_spec=grid_spec,
        out_shape=jax.ShapeDtypeStruct((n, w), src.dtype),
        compiler_params=_cp("arbitrary"),
        name="gather_rows",
    )(idx, src)


def _gate_up_kernel(half, se_ref, sj_ref, sb_ref, so_ref, sf_ref, nv_ref, x_ref, wg_ref, wu_ref,
                    o_ref, wg_s, wu_s):
    s = pl.program_id(0)

    @pl.when(s >= nv_ref[0])
    def _():
        o_ref[...] = jnp.zeros_like(o_ref)

    @pl.when(sf_ref[s] == 1)
    def _():
        wg_s[...] = wg_ref[0, 0].astype(BF16)
        wu_s[...] = wu_ref[0, 0].astype(BF16)

    @pl.when(s < nv_ref[0])
    def _():
        p = x_ref[...]
        lo = _unpack_lo(p).astype(BF16)
        hi = _unpack_hi(p).astype(BF16)
        gt = _dot(lo, wg_s[:half]) + _dot(hi, wg_s[half:])
        up = _dot(lo, wu_s[:half]) + _dot(hi, wu_s[half:])
        o_ref[...] = (_silu(gt) * up).astype(BF16)


def gate_up(x_packed, w_gate, w_up, layer, steps, tm, tn):
    n_slots, half = x_packed.shape
    _, _, d, f = w_gate.shape
    n_steps = steps[0].shape[0]
    w_spec = pl.BlockSpec((1, 1, d, tn), lambda s, se, sj, sb, so, sf, nv: (layer, se[s], 0, sj[s]))
    grid_spec = pltpu.PrefetchScalarGridSpec(
        num_scalar_prefetch=6,
        grid=(n_steps,),
        in_specs=[pl.BlockSpec((tm, half), lambda s, se, sj, sb, so, sf, nv: (sb[s], 0)),
                  w_spec, w_spec],
        out_specs=pl.BlockSpec((tm, tn), lambda s, se, sj, sb, so, sf, nv: (sb[s], so[s])),
        scratch_shapes=[pltpu.VMEM((d, tn), BF16), pltpu.VMEM((d, tn), BF16)])
    return pl.pallas_call(
        functools.partial(_gate_up_kernel, half),
        grid_spec=grid_spec,
        out_shape=jax.ShapeDtypeStruct((n_slots, f), BF16),
        compiler_params=_cp("arbitrary"),
        name="gate_up",
    )(*steps, x_packed, w_gate, w_up)


def _down_kernel(se_ref, sj_ref, sb_ref, so_ref, sf_ref, nv_ref, h_ref, w_ref, o_ref, w_s):
    s = pl.program_id(0)

    @pl.when(s >= nv_ref[0])
    def _():
        o_ref[...] = jnp.zeros_like(o_ref)

    @pl.when(sf_ref[s] == 1)
    def _():
        w_s[...] = w_ref[0, 0].astype(BF16)

    @pl.when(s < nv_ref[0])
    def _():
        y = _dot(h_ref[...], w_s[...])
        hw = y.shape[1] // 2
        o_ref[...] = _pack_words(y[:, :hw], y[:, hw:])


def down(h1, w_down, layer, steps, tm, tn):
    n_slots, f = h1.shape
    d = w_down.shape[3]
    n_steps = steps[0].shape[0]
    grid_spec = pltpu.PrefetchScalarGridSpec(
        num_scalar_prefetch=6,
        grid=(n_steps,),
        in_specs=[pl.BlockSpec((tm, f), lambda s, se, sj, sb, so, sf, nv: (sb[s], 0)),
                  pl.BlockSpec((1, 1, f, tn),
                               lambda s, se, sj, sb, so, sf, nv: (layer, se[s], 0, sj[s]))],
        out_specs=pl.BlockSpec((tm, tn // 2), lambda s, se, sj, sb, so, sf, nv: (sb[s], so[s])),
        scratch_shapes=[pltpu.VMEM((f, tn), BF16)])
    return pl.pallas_call(
        _down_kernel,
        grid_spec=grid_spec,
        out_shape=jax.ShapeDtypeStruct((n_slots, d // 2), U32),
        compiler_params=_cp("arbitrary"),
        name="down",
    )(*steps, h1, w_down)


def _combine_kernel(npb, tb, tn, slot_ref, y_hbm, w_ref, sh_ref, x_ref, gp_ref, gs_ref, o_ref,
                    buf, sem):
    i = pl.program_id(0)
    n = pl.num_programs(0)

    def copy(blk, r, k, slot):
        tok = blk * tb + r
        return pltpu.make_async_copy(y_hbm.at[pl.ds(slot_ref[tok * TOP_K + k], 1)],
                                     buf.at[slot, k, pl.ds(r, 1)], sem.at[slot])

    def start_block(blk, slot):
        def body(r, c):
            for k in range(TOP_K):
                copy(blk, r, k, slot).start()
            return c
        lax.fori_loop(0, tb, body, 0)

    def wait_block(blk, slot):
        def body(r, c):
            for k in range(TOP_K):
                copy(blk, r, k, slot).wait()
            return c
        lax.fori_loop(0, tb, body, 0)

    @pl.when(i == 0)
    def _():
        start_block(0, 0)

    @pl.when(i + 1 < n)
    def _():
        start_block(i + 1, (i + 1) % 2)

    cur = i % 2
    wait_block(i, cur)

    smp = i >= npb
    gate = _pick(smp, gp_ref, gs_ref)
    w = w_ref[...]
    sh = sh_ref[...]
    hw = tn // 2
    d = o_ref.shape[1]
    for c in range(d // tn):
        lo = _unpack_lo(sh[:, c * hw:(c + 1) * hw])
        hi = _unpack_hi(sh[:, c * hw:(c + 1) * hw])
        for k in range(TOP_K):
            p = buf[cur, k, :, c * hw:(c + 1) * hw]
            wk = w[:, k:k + 1]
            lo = lo + wk * _unpack_lo(p)
            hi = hi + wk * _unpack_hi(p)
        a = c * tn
        o_ref[:, a:a + hw] = x_ref[:, a:a + hw] + gate[:, a:a + hw] * lo
        o_ref[:, a + hw:a + tn] = x_ref[:, a + hw:a + tn] + gate[:, a + hw:a + tn] * hi


def combine(y_sorted, slots_flat, wts, y_shared, x, mods, k_gate, tn, tb=64):
    t, d = x.shape
    half = d // 2
    row = lambda i, sl: i
    grid_spec = pltpu.PrefetchScalarGridSpec(
        num_scalar_prefetch=1,
        grid=(t // tb,),
        in_specs=[pl.BlockSpec(memory_space=pl.ANY),
                  pl.BlockSpec((tb, TOP_K), lambda i, sl: (i, 0)),
                  pl.BlockSpec((tb, half), lambda i, sl: (i, 0)),
                  pl.BlockSpec((tb, d), lambda i, sl: (i, 0)),
                  *mods.specs(k_gate, tb, row)],
        out_specs=pl.BlockSpec((tb, d), lambda i, sl: (i, 0)),
        scratch_shapes=[pltpu.VMEM((2, TOP_K, tb, half), U32), pltpu.SemaphoreType.DMA((2,))])
    return pl.pallas_call(
        functools.partial(_combine_kernel, mods.tp // tb, tb, tn),
        grid_spec=grid_spec,
        out_shape=jax.ShapeDtypeStruct((t, d), F32),
        compiler_params=_cp("arbitrary"),
        name="combine",
    )(slots_flat, y_sorted, wts, y_shared, x, *mods.args())


def _dispatch_tables(idx, n_exp, tm, n_tiles):
    t, k = idx.shape
    a = t * k
    flat_e = idx.reshape(a)
    onehot = (flat_e[:, None] == jnp.arange(n_exp, dtype=I32)[None, :]).astype(I32)
    csum = jnp.cumsum(onehot, axis=0)
    counts = csum[-1]
    rank = jnp.take_along_axis(csum, flat_e[:, None], axis=1)[:, 0] - 1
    nblk = (counts + tm - 1) // tm
    blk_end = jnp.cumsum(nblk)
    blk_start = blk_end - nblk
    slot = blk_start[flat_e] * tm + rank
    n_blocks = a // tm + n_exp
    n_slots = n_blocks * tm
    slot_tok = jnp.zeros((n_slots,), I32).at[slot].set(jnp.arange(a, dtype=I32) // k,
                                                     unique_indices=True)
    steps = _grouped_steps(nblk, blk_start, n_blocks, n_tiles)
    return slot, slot_tok, steps, n_slots


def _grouped_steps(nblk, blk_start, n_blocks, n_tiles):
    n_steps = n_blocks * n_tiles
    step_end = jnp.cumsum(nblk) * n_tiles
    total = step_end[-1]
    sidx = jnp.arange(n_steps, dtype=I32)
    valid = sidx < total
    s = jnp.minimum(sidx, total - 1)
    e = jnp.searchsorted(step_end, s, side='right').astype(I32)
    r = s - (step_end[e] - nblk[e] * n_tiles)
    j = r // nblk[e]
    i = r % nblk[e]
    spare = sidx - total
    blk = jnp.where(valid, blk_start[e] + i, jnp.sum(nblk) + spare // n_tiles)
    j_out = jnp.where(valid, j, spare % n_tiles)
    first = jnp.where((i == 0) & valid, 1, 0)
    as_i32 = lambda a: a.astype(I32)
    return (as_i32(e), as_i32(j), as_i32(blk), as_i32(j_out), as_i32(first),
            as_i32(total.reshape(1)))


def _dense_steps(n_blocks, n_tiles):
    s = jnp.arange(n_blocks * n_tiles, dtype=I32)
    j = s // n_blocks
    i = s % n_blocks
    return (jnp.zeros_like(s), j, i, j, (i == 0).astype(I32),
            jnp.full((1,), n_blocks * n_tiles, I32))


def _pack_rows(h):
    half = h.shape[1] // 2
    return lax.bitcast_convert_type(jnp.stack([h[:, :half], h[:, half:]], axis=-1), U32)


def moe_ffn(x, g, mods, layer, router_wt, router_bias, w_gate, w_up, w_down, sw_gate, sw_up,
            sw_down, tl):
    t, d = x.shape
    _, n_exp, _, f = w_gate.shape
    tm, tn_up, tn_down = tl["moe_tm"], tl["tn_up"], tl["tn_down"]
    h, idx_t, wts_t = ffn_in(x, g, mods, router_wt, router_bias, tm=tl["ffn_tm"])
    idx = idx_t.T
    wts = wts_t.T
    hp = _pack_rows(h)
    n_up, n_dn = f // tn_up, d // tn_down
    slot, slot_tok, steps_up, n_slots = _dispatch_tables(idx, n_exp, tm, n_up)
    steps_dn = steps_up if n_dn == n_up else _dispatch_tables(idx, n_exp, tm, n_dn)[2]
    xs = gather_rows(hp, slot_tok, tl["gather_rows"])
    h1 = gate_up(xs, w_gate, w_up, layer, steps_up, tm, tn_up)
    ys = down(h1, w_down, layer, steps_dn, tm, tn_down)
    sh_steps_up = _dense_steps(t // tm, sw_gate.shape[3] // tn_up)
    sh_steps_dn = _dense_steps(t // tm, d // tn_down)
    s1 = gate_up(hp, sw_gate, sw_up, layer, sh_steps_up, tm, tn_up)
    ysh = down(s1, sw_down, layer, sh_steps_dn, tm, tn_down)
    return combine(ys, slot, wts, ysh, x, mods, 5, tn_down, tb=tl["comb_tb"])


def _rope_tables(pos):
    half = QK_ROPE // 2
    inv_freq = ROPE_THETA ** (-jnp.arange(half, dtype=F32) / half)
    ang = pos.astype(F32)[:, None] * inv_freq[None, :]
    z = jnp.zeros((pos.shape[0], LANES - QK_ROPE), F32)
    cos = jnp.concatenate([jnp.cos(ang), jnp.cos(ang), z], axis=1)
    sin = jnp.concatenate([jnp.sin(ang), jnp.sin(ang), z], axis=1)
    return cos, sin


def _rot_cols(w):
    half = w.shape[-1] // 2
    return jnp.concatenate([-w[..., half:], w[..., :half]], axis=-1)


def _pad_cols(w, n):
    return jnp.concatenate([w, jnp.zeros(w.shape[:-1] + (n - w.shape[-1],), w.dtype)], axis=-1)


TILES = dict(adaln_tn=512, norm_tm=256, mla_tm=256, qup_tm=512, kv_tm=512, kv_tn=1024,
             proj_tm=256, proj_tn=512, tq=1024, tk=1024, n_pg=16, pool_ts=256, pool_seqs=8,
             ffn_tm=256, gather_rows=512, moe_tm=256, tn_up=512, tn_down=2048, comb_tb=64)


def kernel(x_prompt, x_sample, cache_kv_latent, cache_k_rope, state_pool, page_table, c_prompt,
           c_sample, ada_w, ada_b, norm_mix_g, norm_ffn_g, mla_w_in, mla_q_norm_g, mla_w_uq,
           mla_kv_norm_g, mla_w_uk, mla_w_uv, mla_w_o, pool_w, pool_scale, router_w, router_bias,
           exp_w_gate, exp_w_up, exp_w_down, shared_w_gate, shared_w_up, shared_w_down,
           final_norm_g):
    return _step(TILES, x_prompt, x_sample, cache_kv_latent, cache_k_rope, state_pool, page_table,
                 c_prompt, c_sample, ada_w, ada_b, norm_mix_g, norm_ffn_g, mla_w_in, mla_q_norm_g,
                 mla_w_uq, mla_kv_norm_g, mla_w_uk, mla_w_uv, mla_w_o, pool_w, pool_scale,
                 router_w, router_bias, exp_w_gate, exp_w_up, exp_w_down, shared_w_gate,
                 shared_w_up, shared_w_down, final_norm_g)


def _step(tl, x_prompt, x_sample, cache_kv_latent, cache_k_rope, state_pool, page_table, c_prompt,
          c_sample, ada_w, ada_b, norm_mix_g, norm_ffn_g, mla_w_in, mla_q_norm_g, mla_w_uq,
          mla_kv_norm_g, mla_w_uk, mla_w_uv, mla_w_o, pool_w, pool_scale, router_w, router_bias,
          exp_w_gate, exp_w_up, exp_w_down, shared_w_gate, shared_w_up, shared_w_down,
          final_norm_g):
    n_p, seq, d = x_prompt.shape
    n_s, dec_seq, _ = x_sample.shape
    depth = ada_w.shape[0]
    tp, ts_tok = n_p * seq, n_s * dec_seq
    t = tp + ts_tok
    page = cache_kv_latent.shape[2]
    past = page_table.shape[1] * page
    q_lora = mla_q_norm_g.shape[1]
    kv_lora = mla_kv_norm_g.shape[1]
    n_h = N_HEADS
    sm_scale = (QK_NOPE + QK_ROPE) ** -0.5
    n_pool_hist = max(POOL_WINDOWS) - 1

    x = jnp.concatenate([x_prompt.reshape(tp, d), x_sample.reshape(ts_tok, d)], axis=0)

    n_c = n_p + n_s
    n_c_pad = -(-n_c // 8) * 8
    c_all = jnp.concatenate([c_prompt, c_sample, jnp.zeros((n_c_pad - n_c, d), F32)], axis=0)
    mod_all = adaln(c_all, ada_w, ada_b, tl["adaln_tn"])

    pos = jnp.concatenate([jnp.tile(jnp.arange(seq), n_p),
                           jnp.tile(past + jnp.arange(dec_seq), n_s)])
    cos_t, sin_t = _rope_tables(pos)

    lat_p, rope_p, pool_p, lat_s, rope_s, pool_s = [], [], [], [], [], []
    for i in range(depth):
        j = i // 2
        mods = Mods(mod_all[i, :n_p].reshape(n_p, 1, 6 * d),
                    jnp.repeat(mod_all[i, n_p:n_c], dec_seq, axis=0), seq, tp, d)
        if i % 2 == 0:
            w_in = mla_w_in[j]
            w_r = w_in[:, q_lora + kv_lora:]
            w_aug = jnp.concatenate([w_in[:, :q_lora + kv_lora], _pad_cols(w_r, LANES),
                                     _pad_cols(_rot_cols(w_r), LANES)], axis=1).astype(BF16)
            cq, ckv, ckv_b, kr, kr_b = mla_in(x, norm_mix_g[i], mods, w_aug, mla_q_norm_g[j],
                                              mla_kv_norm_g[j], cos_t, sin_t, q_lora, kv_lora,
                                              tm=tl["mla_tm"])
            wq = mla_w_uq[j].reshape(q_lora, n_h, QK_NOPE + QK_ROPE)
            wq_r = wq[:, :, QK_NOPE:]
            w1 = jnp.concatenate([wq, jnp.zeros((q_lora, n_h, HEAD_PAD - QK_NOPE - QK_ROPE), F32)],
                                 axis=2).reshape(q_lora, n_h * HEAD_PAD).astype(BF16)
            w2 = _pad_cols(_rot_cols(wq_r), LANES).reshape(q_lora, n_h * LANES).astype(BF16)
            q_cat = q_up(cq, w1, w2, cos_t, sin_t, tm=tl["qup_tm"])

            w_kv = jnp.concatenate([mla_w_uk[j], mla_w_uv[j]], axis=1).astype(BF16)
            kv = matmul(ckv_b[:tp], w_kv, BF16, tl["kv_tm"], tl["kv_tn"])
            attn_p = prompt_attention(q_cat, kv, kr_b, n_p, seq, sm_scale, tl["tq"], tl["tk"])

            w_uk_t = mla_w_uk[j].reshape(kv_lora, n_h, QK_NOPE).transpose(1, 2, 0).astype(BF16)
            w_uv_h = mla_w_uv[j].reshape(kv_lora, n_h, V_HEAD).transpose(1, 0, 2).astype(BF16)
            q_s = q_cat[tp:].reshape(ts_tok, n_h, HEAD_PAD)
            q_nope_s = q_s[:, :, :QK_NOPE].transpose(1, 0, 2)
            q_lat = heads_matmul(
                q_nope_s, w_uk_t,
                pl.BlockSpec((1, ts_tok, QK_NOPE), lambda h: (h, 0, 0)),
                pl.BlockSpec((1, ts_tok, kv_lora), lambda h: (h, 0, 0)),
                jax.ShapeDtypeStruct((n_h, ts_tok, kv_lora), BF16))
            q_lat = q_lat.transpose(1, 0, 2).reshape(n_s, dec_seq * n_h, kv_lora)
            q_rope_s = q_s[:, :, QK_NOPE:QK_NOPE + QK_ROPE].reshape(n_s, dec_seq * n_h, QK_ROPE)
            kpad = page - dec_seq
            new_lat = jnp.pad(ckv_b[tp:].reshape(n_s, dec_seq, kv_lora), ((0, 0), (0, kpad), (0, 0)))
            new_rope = jnp.pad(kr_b[tp:, :QK_ROPE].reshape(n_s, dec_seq, QK_ROPE),
                               ((0, 0), (0, kpad), (0, 0)))
            o_lat = sample_attention(q_lat, q_rope_s, cache_kv_latent, cache_k_rope, j,
                                     page_table, new_lat, new_rope, dec_seq, sm_scale,
                                     n_pg=tl["n_pg"])
            o_lat = o_lat.reshape(ts_tok, n_h, kv_lora).transpose(1, 0, 2)
            attn_s = heads_matmul(
                o_lat, w_uv_h,
                pl.BlockSpec((1, ts_tok, kv_lora), lambda h: (h, 0, 0)),
                pl.BlockSpec((ts_tok, V_HEAD), lambda h: (0, h)),
                jax.ShapeDtypeStruct((ts_tok, n_h * V_HEAD), BF16))
            attn = jnp.concatenate([attn_p, attn_s], axis=0)
            x = proj_residual(attn, mla_w_o[j].astype(BF16), x, mods, 2, tl["proj_tm"],
                              tl["proj_tn"])

            lat_p.append(ckv[:tp].reshape(n_p, seq, kv_lora))
            rope_p.append(kr[:tp, :QK_ROPE].reshape(n_p, seq, QK_ROPE))
            lat_s.append(ckv[tp:].reshape(n_s, dec_seq, kv_lora))
            rope_s.append(kr[tp:, :QK_ROPE].reshape(n_s, dec_seq, QK_ROPE))
        else:
            h = modnorm(x, norm_mix_g[i], mods, 0, 1, F32, tm=tl["norm_tm"])
            pw = pool_w[j].astype(BF16)
            hp3 = h[:tp].reshape(n_p, seq, d)
            gate_p = mods.mp[:, :, 2 * d:3 * d]
            xp_new = pool_mixer(hp3, x[:tp].reshape(n_p, seq, d), gate_p, pw, pool_scale[j], 0,
                                tl["pool_ts"])
            hs3 = h[tp:].reshape(n_s, dec_seq, d)
            rows = POOL_HALO + 8
            tail = rows - POOL_HALO - dec_seq
            lead = POOL_HALO - n_pool_hist
            hh = jnp.concatenate([jnp.zeros((n_s, lead, d), F32), state_pool[j], hs3,
                                  jnp.zeros((n_s, tail, d), F32)], axis=1)
            pad3 = lambda a: jnp.pad(a, ((0, 0), (POOL_HALO, tail), (0, 0)))
            xs3 = x[tp:].reshape(n_s, dec_seq, d)
            gate_s = mods.ms[:, 2 * d:3 * d].reshape(n_s, dec_seq, d)
            seqs = tl["pool_seqs"]
            xs_new = pool_mixer(hh.reshape(n_s // seqs, seqs * rows, d),
                                pad3(xs3).reshape(n_s // seqs, seqs * rows, d),
                                pad3(gate_s).reshape(n_s // seqs, seqs * rows, d),
                                pw, pool_scale[j], past - POOL_HALO, seqs * rows, period=rows)
            xs_new = xs_new.reshape(n_s, rows, d)[:, POOL_HALO:POOL_HALO + dec_seq]
            x = jnp.concatenate([xp_new.reshape(tp, d), xs_new.reshape(ts_tok, d)], axis=0)
            pool_p.append(hp3[:, seq - n_pool_hist:])
            pool_s.append(jnp.concatenate([state_pool[j], hs3], axis=1)[:, dec_seq:])

        x = moe_ffn(x, norm_ffn_g[i], mods, i, router_w[i].T, router_bias[i], exp_w_gate,
                    exp_w_up, exp_w_down, shared_w_gate[:, None], shared_w_up[:, None],
                    shared_w_down[:, None], tl)

    y = rmsnorm(x, final_norm_g, tm=tl["norm_tm"])
    return (y[:tp].reshape(n_p, seq, d), y[tp:].reshape(n_s, dec_seq, d),
            jnp.stack(lat_p), jnp.stack(rope_p), jnp.stack(pool_p),
            jnp.stack(lat_s), jnp.stack(rope_s), jnp.stack(pool_s))
```

```python
import functools

import jax
import jax.numpy as jnp
from jax import lax
from jax.experimental import pallas as pl
from jax.experimental.pallas import tpu as pltpu

F32 = jnp.float32
BF16 = jnp.bfloat16
U32 = jnp.uint32
I32 = jnp.int32

EPS = 1e-6
ROPE_THETA = 10000.0
ROUTED_SCALE = 2.5
N_GROUPS = 8
TOPK_GROUPS = 4
TOP_K = 8
POOL_WINDOWS = (2, 4, 8, 16)
POOL_HALO = 16
N_HEADS = 32
QK_NOPE = 128
QK_ROPE = 64
V_HEAD = 128
LANES = 128
HEAD_PAD = 2 * LANES
VMEM_LIMIT = 56 * 1024 * 1024
COMBINE_CHUNK = 256
NEG_BIG = -1e30
LOG2_E = 1.4426950408889634


def _cp(*sem, vmem=VMEM_LIMIT):
    return pltpu.CompilerParams(dimension_semantics=sem, vmem_limit_bytes=vmem)


def _dot(a, b):
    return jnp.dot(a, b, preferred_element_type=F32)


def _dot_nt(a, b):
    return lax.dot_general(a, b, (((1,), (1,)), ((), ())), preferred_element_type=F32)


def _rms(x, g):
    return x * lax.rsqrt(jnp.mean(x * x, axis=-1, keepdims=True) + EPS) * g


def _silu(x):
    return x * jax.nn.sigmoid(x)


def _unpack_lo(p):
    return lax.bitcast_convert_type(p << 16, F32)


def _unpack_hi(p):
    return lax.bitcast_convert_type(p & jnp.uint32(0xFFFF0000), F32)


def _pack_words(lo, hi):
    lo_b = lax.bitcast_convert_type(lo.astype(BF16).astype(F32), U32)
    hi_b = lax.bitcast_convert_type(hi.astype(BF16).astype(F32), U32)
    return (hi_b & jnp.uint32(0xFFFF0000)) | (lo_b >> 16)


def _adaln_kernel(c_ref, w_ref, b_ref, o_ref):
    s = _silu(c_ref[...]).astype(BF16)
    o_ref[0] = _dot(s, w_ref[0].astype(BF16)) + b_ref[0]


def adaln(c_all, ada_w, ada_b, tn=512):
    depth, d, n = ada_w.shape
    r = c_all.shape[0]
    return pl.pallas_call(
        _adaln_kernel,
        grid=(depth, n // tn),
        in_specs=[pl.BlockSpec((r, d), lambda l, j: (0, 0)),
                  pl.BlockSpec((1, d, tn), lambda l, j: (l, 0, j)),
                  pl.BlockSpec((1, 1, tn), lambda l, j: (l, 0, j))],
        out_specs=pl.BlockSpec((1, r, tn), lambda l, j: (l, 0, j)),
        out_shape=jax.ShapeDtypeStruct((depth, r, n), F32),
        compiler_params=_cp("arbitrary", "arbitrary"),
        name="adaln",
    )(c_all, ada_w, ada_b.reshape(depth, 1, n))


class Mods:
    def __init__(self, mp, ms, seq, n_prompt_tok, d):
        self.mp, self.ms, self.seq, self.tp, self.d = mp, ms, seq, n_prompt_tok, d

    def specs(self, k, tm, row_of):
        npb = self.tp // tm
        per_seq = self.seq // tm
        n_p = self.mp.shape[0]

        def p_map(*g):
            return (jnp.minimum(row_of(*g) // per_seq, n_p - 1), 0, k)

        def s_map(*g):
            return (jnp.maximum(row_of(*g) - npb, 0), k)

        return [pl.BlockSpec((1, 1, self.d), p_map), pl.BlockSpec((tm, self.d), s_map)]

    def args(self):
        return [self.mp, self.ms]


def _pick(is_sample, p_ref, s_ref):
    return jnp.where(is_sample, s_ref[...], p_ref[0])


def _modnorm_kernel(npb, x_ref, g_ref, shp_ref, shs_ref, scp_ref, scs_ref, o_ref):
    smp = pl.program_id(0) >= npb
    shift = _pick(smp, shp_ref, shs_ref)
    scale = _pick(smp, scp_ref, scs_ref)
    o_ref[...] = (_rms(x_ref[...], g_ref[...]) * (1.0 + scale) + shift).astype(o_ref.dtype)


def modnorm(x, g, mods, k_shift, k_scale, out_dtype, tm=256):
    t, d = x.shape
    row = lambda i: i
    return pl.pallas_call(
        functools.partial(_modnorm_kernel, mods.tp // tm),
        grid=(t // tm,),
        in_specs=[pl.BlockSpec((tm, d), lambda i: (i, 0)),
                  pl.BlockSpec((1, d), lambda i: (0, 0)),
                  *mods.specs(k_shift, tm, row), *mods.specs(k_scale, tm, row)],
        out_specs=pl.BlockSpec((tm, d), lambda i: (i, 0)),
        out_shape=jax.ShapeDtypeStruct((t, d), out_dtype),
        compiler_params=_cp("arbitrary"),
        name="modnorm",
    )(x, g.reshape(1, d), *mods.args(), *mods.args())


def _rmsnorm_kernel(npb, x_ref, g_ref, op_ref, os_ref):
    i = pl.program_id(0)
    y = _rms(x_ref[...], g_ref[...])

    @pl.when(i < npb)
    def _():
        op_ref[...] = y

    @pl.when(i >= npb)
    def _():
        os_ref[...] = y


def rmsnorm_split(x, g, tp, tm=256):
    t, d = x.shape
    npb = tp // tm
    return pl.pallas_call(
        functools.partial(_rmsnorm_kernel, npb),
        grid=(t // tm,),
        in_specs=[pl.BlockSpec((tm, d), lambda i: (i, 0)), pl.BlockSpec((1, d), lambda i: (0, 0))],
        out_specs=[pl.BlockSpec((tm, d), lambda i: (jnp.minimum(i, npb - 1), 0)),
                   pl.BlockSpec((tm, d), lambda i: (jnp.maximum(i - npb, 0), 0))],
        out_shape=[jax.ShapeDtypeStruct((tp, d), F32), jax.ShapeDtypeStruct((t - tp, d), F32)],
        compiler_params=_cp("arbitrary"),
        name="final_rmsnorm",
    )(x, g.reshape(1, d))


def _mla_in_kernel(npb, q_lora, kv_lora, x_ref, g_ref, shp_ref, shs_ref, scp_ref, scs_ref,
                   w_ref, gq_ref, gkv_ref, cos_ref, sin_ref,
                   cq_ref, ckv_ref, ckvb_ref, kr_ref, krb_ref):
    smp = pl.program_id(0) >= npb
    shift = _pick(smp, shp_ref, shs_ref)
    scale = _pick(smp, scp_ref, scs_ref)
    h = (_rms(x_ref[...], g_ref[...]) * (1.0 + scale) + shift).astype(BF16)
    a = _dot(h, w_ref[...])
    cq_ref[...] = _rms(a[:, :q_lora], gq_ref[...]).astype(BF16)
    ckv = _rms(a[:, q_lora:q_lora + kv_lora], gkv_ref[...])
    ckv_ref[...] = ckv
    ckvb_ref[...] = ckv.astype(BF16)
    o = q_lora + kv_lora
    kr = a[:, o:o + LANES] * cos_ref[...] + a[:, o + LANES:o + 2 * LANES] * sin_ref[...]
    kr_ref[...] = kr
    krb_ref[...] = kr.astype(BF16)


def mla_in(x, g, mods, w_aug, gq, gkv, cos_t, sin_t, q_lora, kv_lora, tm=256):
    t, d = x.shape
    n_aug = w_aug.shape[1]
    row = lambda i: i
    blk = lambda w: pl.BlockSpec((tm, w), lambda i: (i, 0))
    return pl.pallas_call(
        functools.partial(_mla_in_kernel, mods.tp // tm, q_lora, kv_lora),
        grid=(t // tm,),
        in_specs=[blk(d), pl.BlockSpec((1, d), lambda i: (0, 0)),
                  *mods.specs(0, tm, row), *mods.specs(1, tm, row),
                  pl.BlockSpec((d, n_aug), lambda i: (0, 0)),
                  pl.BlockSpec((1, q_lora), lambda i: (0, 0)),
                  pl.BlockSpec((1, kv_lora), lambda i: (0, 0)),
                  blk(LANES), blk(LANES)],
        out_specs=[blk(q_lora), blk(kv_lora), blk(kv_lora), blk(LANES), blk(LANES)],
        out_shape=[jax.ShapeDtypeStruct((t, q_lora), BF16),
                   jax.ShapeDtypeStruct((t, kv_lora), F32),
                   jax.ShapeDtypeStruct((t, kv_lora), BF16),
                   jax.ShapeDtypeStruct((t, LANES), F32),
                   jax.ShapeDtypeStruct((t, LANES), BF16)],
        compiler_params=_cp("arbitrary"),
        name="mla_in",
    )(x, g.reshape(1, d), *mods.args(), *mods.args(), w_aug, gq.reshape(1, -1),
      gkv.reshape(1, -1), cos_t, sin_t)


def _q_up_kernel(hg, cq_ref, w1_ref, w2_ref, cos_ref, sin_ref, o_ref):
    cq = cq_ref[...]
    a = _dot(cq, w1_ref[...])
    r = _dot(cq, w2_ref[...])
    c = cos_ref[...]
    s = sin_ref[...]
    for h in range(hg):
        b = h * HEAD_PAD
        o_ref[:, b:b + LANES] = a[:, b:b + LANES].astype(BF16)
        o_ref[:, b + LANES:b + HEAD_PAD] = (
            a[:, b + LANES:b + HEAD_PAD] * c + r[:, h * LANES:(h + 1) * LANES] * s).astype(BF16)


def q_up(cq, w1, w2, cos_t, sin_t, tm=512, hg=4):
    t, k = cq.shape
    n_h = w1.shape[1] // HEAD_PAD
    return pl.pallas_call(
        functools.partial(_q_up_kernel, hg),
        grid=(n_h // hg, t // tm),
        in_specs=[pl.BlockSpec((tm, k), lambda j, i: (i, 0)),
                  pl.BlockSpec((k, hg * HEAD_PAD), lambda j, i: (0, j)),
                  pl.BlockSpec((k, hg * LANES), lambda j, i: (0, j)),
                  pl.BlockSpec((tm, LANES), lambda j, i: (i, 0)),
                  pl.BlockSpec((tm, LANES), lambda j, i: (i, 0))],
        out_specs=pl.BlockSpec((tm, hg * HEAD_PAD), lambda j, i: (i, j)),
        out_shape=jax.ShapeDtypeStruct((t, n_h * HEAD_PAD), BF16),
        compiler_params=_cp("arbitrary", "arbitrary"),
        name="q_up",
    )(cq, w1, w2, cos_t, sin_t)


def _mm_kernel(a_ref, b_ref, o_ref):
    o_ref[...] = _dot(a_ref[...], b_ref[...]).astype(o_ref.dtype)


def matmul(a, b, out_dtype, tm=512, tn=1024):
    m, k = a.shape
    n = b.shape[1]
    tm, tn = min(tm, m), min(tn, n)
    return pl.pallas_call(
        _mm_kernel,
        grid=(n // tn, m // tm),
        in_specs=[pl.BlockSpec((tm, k), lambda j, i: (i, 0)),
                  pl.BlockSpec((k, tn), lambda j, i: (0, j))],
        out_specs=pl.BlockSpec((tm, tn), lambda j, i: (i, j)),
        out_shape=jax.ShapeDtypeStruct((m, n), out_dtype),
        compiler_params=_cp("arbitrary", "arbitrary"),
        name="matmul",
    )(a, b)


def _proj_res_kernel(npb, a_ref, w_ref, x_ref, gp_ref, gs_ref, o_ref):
    smp = pl.program_id(1) >= npb
    gate = _pick(smp, gp_ref, gs_ref)
    o_ref[...] = x_ref[...] + gate * _dot(a_ref[...], w_ref[...])


def proj_residual(a, w, x, mods, k_gate, tm=256, tn=512):
    t, k = a.shape
    d = w.shape[1]
    npb = mods.tp // tm
    per_seq = mods.seq // tm
    n_p = mods.mp.shape[0]
    cb = mods.d // tn
    gp = pl.BlockSpec((1, 1, tn), lambda j, i: (jnp.minimum(i // per_seq, n_p - 1), 0, k_gate * cb + j))
    gs = pl.BlockSpec((tm, tn), lambda j, i: (jnp.maximum(i - npb, 0), k_gate * cb + j))
    return pl.pallas_call(
        functools.partial(_proj_res_kernel, npb),
        grid=(d // tn, t // tm),
        in_specs=[pl.BlockSpec((tm, k), lambda j, i: (i, 0)),
                  pl.BlockSpec((k, tn), lambda j, i: (0, j)),
                  pl.BlockSpec((tm, tn), lambda j, i: (i, j)), gp, gs],
        out_specs=pl.BlockSpec((tm, tn), lambda j, i: (i, j)),
        out_shape=jax.ShapeDtypeStruct((t, d), F32),
        compiler_params=_cp("arbitrary", "arbitrary"),
        name="proj_residual",
    )(a, w, x, *mods.args())


def _flash_kernel(tq, tk, scale, q_ref, kn_ref, kr_ref, v_ref, o_ref, m_s, l_s, acc_s):
    qi = pl.program_id(2)
    ki = pl.program_id(3)
    nk = pl.num_programs(3)

    @pl.when(ki == 0)
    def _():
        m_s[...] = jnp.full_like(m_s, NEG_BIG)
        l_s[...] = jnp.zeros_like(l_s)
        acc_s[...] = jnp.zeros_like(acc_s)

    def update(masked):
        k = jnp.concatenate([kn_ref[...], kr_ref[...]], axis=1)
        s = _dot_nt(q_ref[...], k) * (scale * LOG2_E)
        if masked:
            qpos = qi * tq + lax.broadcasted_iota(I32, (tq, tk), 0)
            kpos = ki * tk + lax.broadcasted_iota(I32, (tq, tk), 1)
            s = jnp.where(kpos <= qpos, s, NEG_BIG)
        m_old = m_s[...]
        m_new = jnp.maximum(m_old, jnp.max(s, axis=-1, keepdims=True))
        alpha = jnp.exp2(m_old - m_new)
        p = jnp.exp2(s - m_new)
        l_s[...] = alpha * l_s[...] + jnp.sum(p, axis=-1, keepdims=True)
        acc_s[...] = alpha * acc_s[...] + _dot(p.astype(BF16), v_ref[...])
        m_s[...] = m_new

    live = ki * tk <= qi * tq + (tq - 1)
    crosses = ki * tk + (tk - 1) > qi * tq

    @pl.when(jnp.logical_and(live, crosses))
    def _():
        update(True)

    @pl.when(jnp.logical_and(live, jnp.logical_not(crosses)))
    def _():
        update(False)

    @pl.when(ki == nk - 1)
    def _():
        o_ref[...] = (acc_s[...] / l_s[...]).astype(o_ref.dtype)


def prompt_attention(q_cat, kv, kr_b, n_seq, seq, scale, tq=1024, tk=1024):
    n_h = q_cat.shape[1] // HEAD_PAD
    tq, tk = min(tq, seq), min(tk, seq)
    nq, nk = seq // tq, seq // tk

    def kmap(b, h, qi, ki):
        return jnp.minimum(ki, (qi * tq + tq - 1) // tk)

    return pl.pallas_call(
        functools.partial(_flash_kernel, tq, tk, scale),
        grid=(n_seq, n_h, nq, nk),
        in_specs=[pl.BlockSpec((tq, HEAD_PAD), lambda b, h, qi, ki: (b * nq + qi, h)),
                  pl.BlockSpec((tk, QK_NOPE), lambda b, h, qi, ki: (b * nk + kmap(b, h, qi, ki), h)),
                  pl.BlockSpec((tk, LANES), lambda b, h, qi, ki: (b * nk + kmap(b, h, qi, ki), 0)),
                  pl.BlockSpec((tk, V_HEAD), lambda b, h, qi, ki: (b * nk + kmap(b, h, qi, ki), n_h + h))],
        out_specs=pl.BlockSpec((tq, V_HEAD), lambda b, h, qi, ki: (b * nq + qi, h)),
        out_shape=jax.ShapeDtypeStruct((n_seq * seq, n_h * V_HEAD), BF16),
        scratch_shapes=[pltpu.VMEM((tq, 1), F32), pltpu.VMEM((tq, 1), F32),
                        pltpu.VMEM((tq, V_HEAD), F32)],
        compiler_params=_cp("arbitrary", "arbitrary", "arbitrary", "arbitrary"),
        name="prompt_attention",
    )(q_cat, kv, kr_b, kv)


def _bmm_kernel(a_ref, b_ref, o_ref):
    a = a_ref[...]
    a = a.reshape(a.shape[-2:])
    o = _dot(a, b_ref[0]).astype(o_ref.dtype)
    o_ref[...] = o.reshape(o_ref.shape)


def heads_matmul(a, b, a_spec, out_spec, out_shape):
    n_h = b.shape[0]
    return pl.pallas_call(
        _bmm_kernel,
        grid=(n_h,),
        in_specs=[a_spec, pl.BlockSpec((1,) + b.shape[1:], lambda h: (h, 0, 0))],
        out_specs=out_spec,
        out_shape=out_shape,
        compiler_params=_cp("arbitrary"),
        name="heads_matmul",
    )(a, b)


def _decode_kernel(layer, n_pg, n_grp, dec_seq, scale, pt_ref, ql_ref, qr_ref, lat_hbm, rope_hbm,
                   nlat_ref, nrope_ref, o_ref, lat_buf, rope_buf, sem, m_s, l_s, acc_s):
    i = pl.program_id(0)
    g = pl.program_id(1)
    n_seq = pl.num_programs(0)
    page = lat_hbm.shape[2]

    def copies(seq, grp, slot):
        out = []
        for k in range(n_pg):
            pg = pt_ref[seq, grp * n_pg + k]
            out.append(pltpu.make_async_copy(lat_hbm.at[layer, pg],
                                             lat_buf.at[slot, pl.ds(k * page, page)], sem.at[slot]))
            out.append(pltpu.make_async_copy(rope_hbm.at[layer, pg],
                                             rope_buf.at[slot, :, pl.ds(k * page, page)],
                                             sem.at[slot]))
        return out

    step = i * n_grp + g
    slot = step % 2

    @pl.when(jnp.logical_and(i == 0, g == 0))
    def _():
        for c in copies(0, 0, 0):
            c.start()

    @pl.when(jnp.logical_and(g < n_grp, step + 1 < n_seq * n_grp))
    def _():
        nxt = step + 1
        for c in copies(nxt // n_grp, nxt % n_grp, nxt % 2):
            c.start()

    @pl.when(g == 0)
    def _():
        m_s[...] = jnp.full_like(m_s, NEG_BIG)
        l_s[...] = jnp.zeros_like(l_s)
        acc_s[...] = jnp.zeros_like(acc_s)

    def update(lat, rope_t, mask):
        s = (_dot_nt(ql_ref[0], lat) + _dot(qr_ref[0], rope_t)) * (scale * LOG2_E)
        if mask is not None:
            s = jnp.where(mask, s, NEG_BIG)
        m_old = m_s[...]
        m_new = jnp.maximum(m_old, jnp.max(s, axis=-1, keepdims=True))
        alpha = jnp.exp2(m_old - m_new)
        p = jnp.exp2(s - m_new)
        l_s[...] = alpha * l_s[...] + jnp.sum(p, axis=-1, keepdims=True)
        acc_s[...] = alpha * acc_s[...] + _dot(p.astype(BF16), lat)
        m_s[...] = m_new

    @pl.when(g < n_grp)
    def _():
        for c in copies(i, g, slot):
            c.wait()
        update(lat_buf[slot].astype(BF16), rope_buf[slot].astype(BF16), None)

    @pl.when(g == n_grp)
    def _():
        lat = nlat_ref[0]
        rows, keys = ql_ref.shape[1], lat.shape[0]
        q_tok = lax.broadcasted_iota(I32, (rows, keys), 0) // (rows // dec_seq)
        k_tok = lax.broadcasted_iota(I32, (rows, keys), 1)
        update(lat, nrope_ref[0], k_tok <= q_tok)
        o_ref[0] = (acc_s[...] / l_s[...]).astype(o_ref.dtype)


def sample_attention(q_lat, q_rope, cache_lat, cache_rope, layer, page_table, new_lat, new_rope,
                     dec_seq, scale, n_pg=16):
    n, r, c = q_lat.shape
    page = cache_lat.shape[2]
    rd = cache_rope.shape[2]
    n_pages = page_table.shape[1]
    n_pg = min(n_pg, n_pages)
    n_grp = n_pages // n_pg
    kn = new_lat.shape[1]

    seq_map = lambda i, g, pt: (i, 0, 0)
    in_specs = [pl.BlockSpec((1, r, c), seq_map), pl.BlockSpec((1, r, rd), seq_map),
                pl.BlockSpec(memory_space=pl.ANY), pl.BlockSpec(memory_space=pl.ANY),
                pl.BlockSpec((1, kn, c), seq_map), pl.BlockSpec((1, rd, kn), seq_map)]
    grid_spec = pltpu.PrefetchScalarGridSpec(
        num_scalar_prefetch=1,
        grid=(n, n_grp + 1),
        in_specs=in_specs,
        out_specs=pl.BlockSpec((1, r, c), seq_map),
        scratch_shapes=[pltpu.VMEM((2, n_pg * page, c), F32), pltpu.VMEM((2, rd, n_pg * page), F32),
                        pltpu.SemaphoreType.DMA((2,)),
                        pltpu.VMEM((r, 1), F32), pltpu.VMEM((r, 1), F32), pltpu.VMEM((r, c), F32)])
    return pl.pallas_call(
        functools.partial(_decode_kernel, layer, n_pg, n_grp, dec_seq, scale),
        grid_spec=grid_spec,
        out_shape=jax.ShapeDtypeStruct((n, r, c), BF16),
        compiler_params=_cp("arbitrary", "arbitrary"),
        name="sample_attention",
    )(page_table, q_lat, q_rope, cache_lat, cache_rope, new_lat, new_rope)


def _pool_kernel(ts, grp_w, first_pos, period, h_ref, halo_ref, x_ref, gate_ref, w_ref, ps_ref,
                 o_ref):
    i = pl.program_id(1)
    keep = jnp.where(i > 0, 1.0, 0.0).astype(F32)
    u_all = jnp.concatenate([halo_ref[0] * keep, h_ref[0]], axis=0)
    rows = POOL_HALO + ts
    r = i * ts - POOL_HALO + lax.broadcasted_iota(I32, (rows, 1), 0)
    if period:
        r = (r + period) % period
    pos = first_pos + r
    for g, win in enumerate(POOL_WINDOWS):
        u = u_all[:, g * grp_w:(g + 1) * grp_w]
        s = u
        k = 1
        while k < win:
            s = s + pltpu.roll(s, k, axis=0)
            k *= 2
        count = jnp.clip(pos + 1, 1, win).astype(F32)
        dlt = (s / count - u)[POOL_HALO:].astype(BF16)
        y = _dot(dlt, w_ref[g]) * ps_ref[:, g * grp_w:(g + 1) * grp_w]
        o_ref[0, :, g * grp_w:(g + 1) * grp_w] = (
            x_ref[0, :, g * grp_w:(g + 1) * grp_w] + gate_ref[0, :, g * grp_w:(g + 1) * grp_w] * y)


def pool_mixer(h, x, gate, pool_w, pool_scale, first_pos, ts, period=0):
    n, s, d = h.shape
    grp_w = d // len(POOL_WINDOWS)
    hb = ts // POOL_HALO
    if gate.shape[1] == 1:
        gate_spec = pl.BlockSpec((1, 1, d), lambda b, i: (b, 0, 0))
    else:
        gate_spec = pl.BlockSpec((1, ts, d), lambda b, i: (b, i, 0))
    return pl.pallas_call(
        functools.partial(_pool_kernel, ts, grp_w, first_pos, period),
        grid=(n, s // ts),
        in_specs=[pl.BlockSpec((1, ts, d), lambda b, i: (b, i, 0)),
                  pl.BlockSpec((1, POOL_HALO, d), lambda b, i: (b, jnp.maximum(i * hb - 1, 0), 0)),
                  pl.BlockSpec((1, ts, d), lambda b, i: (b, i, 0)),
                  gate_spec,
                  pl.BlockSpec(pool_w.shape, lambda b, i: (0, 0, 0)),
                  pl.BlockSpec((1, d), lambda b, i: (0, 0))],
        out_specs=pl.BlockSpec((1, ts, d), lambda b, i: (b, i, 0)),
        out_shape=jax.ShapeDtypeStruct((n, s, d), F32),
        compiler_params=_cp("arbitrary", "arbitrary"),
        name="pool_mixer",
    )(h, h, x, gate, pool_w, pool_scale.reshape(1, d))


def _ffn_in_kernel(npb, n_exp, x_ref, g_ref, shp_ref, shs_ref, scp_ref, scs_ref, rw_ref, rb_ref,
                   h_ref, idx_ref, wts_ref):
    smp = pl.program_id(0) >= npb
    shift = _pick(smp, shp_ref, shs_ref)
    scale = _pick(smp, scp_ref, scs_ref)
    h = _rms(x_ref[...], g_ref[...]) * (1.0 + scale) + shift
    h_ref[...] = h.astype(BF16)
    logits = lax.dot_general(rw_ref[...], h, (((1,), (1,)), ((), ())),
                             preferred_element_type=F32, precision=lax.Precision.HIGHEST)
    scores = jax.nn.sigmoid(logits)
    sel = scores + rb_ref[...]
    tm = sel.shape[1]
    gsz = n_exp // N_GROUPS
    eid = lax.broadcasted_iota(I32, (n_exp, tm), 0)
    gid_of_e = eid // gsz
    gs = []
    for g in range(N_GROUPS):
        blk = sel[g * gsz:(g + 1) * gsz]
        rid = lax.broadcasted_iota(I32, (gsz, tm), 0)
        m1 = jnp.max(blk, axis=0, keepdims=True)
        a1 = jnp.min(jnp.where(blk == m1, rid, gsz), axis=0, keepdims=True)
        m2 = jnp.max(jnp.where(rid == a1, -jnp.inf, blk), axis=0, keepdims=True)
        gs.append(m1 + m2)
    gs = jnp.concatenate(gs, axis=0)
    grow = lax.broadcasted_iota(I32, (N_GROUPS, tm), 0)
    gm_f = jnp.zeros((N_GROUPS, tm), F32)
    work = gs
    for _ in range(TOPK_GROUPS):
        m = jnp.max(work, axis=0, keepdims=True)
        a = jnp.min(jnp.where(work == m, grow, N_GROUPS), axis=0, keepdims=True)
        hit = grow == a
        gm_f = jnp.where(hit, 1.0, gm_f)
        work = jnp.where(hit, -jnp.inf, work)
    emask = jnp.zeros((n_exp, tm), F32)
    for g in range(N_GROUPS):
        emask = jnp.where(gid_of_e == g, gm_f[g:g + 1], emask)
    work = jnp.where(emask > 0.5, sel, -jnp.inf)
    ids, ws = [], []
    for _ in range(TOP_K):
        m = jnp.max(work, axis=0, keepdims=True)
        a = jnp.min(jnp.where(work == m, eid, n_exp), axis=0, keepdims=True)
        hit = eid == a
        ids.append(a)
        ws.append(jnp.sum(jnp.where(hit, scores, 0.0), axis=0, keepdims=True))
        work = jnp.where(hit, -jnp.inf, work)
    ids = jnp.concatenate(ids, axis=0)
    ws = jnp.concatenate(ws, axis=0)
    idx_ref[...] = ids
    wts_ref[...] = ws / jnp.sum(ws, axis=0, keepdims=True) * ROUTED_SCALE


def ffn_in(x, g, mods, router_wt, router_bias, tm=256):
    t, d = x.shape
    n_exp = router_wt.shape[0]
    row = lambda i: i
    return pl.pallas_call(
        functools.partial(_ffn_in_kernel, mods.tp // tm, n_exp),
        grid=(t // tm,),
        in_specs=[pl.BlockSpec((tm, d), lambda i: (i, 0)), pl.BlockSpec((1, d), lambda i: (0, 0)),
                  *mods.specs(3, tm, row), *mods.specs(4, tm, row),
                  pl.BlockSpec((n_exp, d), lambda i: (0, 0)),
                  pl.BlockSpec((n_exp, 1), lambda i: (0, 0))],
        out_specs=[pl.BlockSpec((tm, d), lambda i: (i, 0)),
                   pl.BlockSpec((TOP_K, tm), lambda i: (0, i)),
                   pl.BlockSpec((TOP_K, tm), lambda i: (0, i))],
        out_shape=[jax.ShapeDtypeStruct((t, d), BF16),
                   jax.ShapeDtypeStruct((TOP_K, t), I32),
                   jax.ShapeDtypeStruct((TOP_K, t), F32)],
        compiler_params=_cp("arbitrary"),
        name="ffn_in",
    )(x, g.reshape(1, d), *mods.args(), *mods.args(), router_wt, router_bias.reshape(n_exp, 1))


def _gather_kernel(rows_per_step, idx_ref, src_ref, o_ref, sem):
    base = pl.program_id(0) * rows_per_step

    def issue(r, c):
        pltpu.make_async_copy(src_ref.at[pl.ds(idx_ref[base + r], 1)],
                              o_ref.at[pl.ds(r, 1)], sem).start()
        return c

    lax.fori_loop(0, rows_per_step, issue, 0, unroll=8)
    pltpu.make_async_copy(src_ref.at[pl.ds(0, rows_per_step)], o_ref, sem).wait()


def gather_rows(src, idx, rows_per_step=512):
    n = idx.shape[0]
    w = src.shape[1]
    grid_spec = pltpu.PrefetchScalarGridSpec(
        num_scalar_prefetch=1,
        grid=(n // rows_per_step,),
        in_specs=[pl.BlockSpec(memory_space=pl.ANY)],
        out_specs=pl.BlockSpec((rows_per_step, w), lambda i, idx: (i, 0)),
        scratch_shapes=[pltpu.SemaphoreType.DMA(())])
    return pl.pallas_call(
        functools.partial(_gather_kernel, rows_per_step),
        grid_spec=grid_spec,
        out_shape=jax.ShapeDtypeStruct((n, w), src.dtype),
        compiler_params=_cp("arbitrary"),
        name="gather_rows",
    )(idx, src)


def _stream_group_weights(s, sf_ref, sg_ref, ng_ref, copies, consume):
    @pl.when(sf_ref[s] == 1)
    def _():
        g = sg_ref[s]

        @pl.when(g == 0)
        def _():
            for c in copies(0):
                c.start()

        for c in copies(g):
            c.wait()
        consume()

        @pl.when(g + 1 < ng_ref[0])
        def _():
            for c in copies(g + 1):
                c.start()


def _gate_up_kernel(layer, half, tn, sb_ref, so_ref, sf_ref, sg_ref, nv_ref, ge_ref, gj_ref, ng_ref,
                    x_ref, wg_hbm, wu_hbm, o_ref, stage, wg_s, wu_s, sem):
    s = pl.program_id(0)

    @pl.when(s >= nv_ref[0])
    def _():
        o_ref[...] = jnp.zeros_like(o_ref)

    def copies(g):
        col = pl.ds(pl.multiple_of(gj_ref[g] * tn, tn), tn)
        return (pltpu.make_async_copy(wg_hbm.at[layer, ge_ref[g], :, col], stage.at[0], sem.at[0]),
                pltpu.make_async_copy(wu_hbm.at[layer, ge_ref[g], :, col], stage.at[1], sem.at[0]))

    def consume():
        wg_s[...] = stage[0].astype(BF16)
        wu_s[...] = stage[1].astype(BF16)

    _stream_group_weights(s, sf_ref, sg_ref, ng_ref, copies, consume)

    @pl.when(s < nv_ref[0])
    def _():
        p = x_ref[...]
        lo = _unpack_lo(p).astype(BF16)
        hi = _unpack_hi(p).astype(BF16)
        gt = _dot(lo, wg_s[:half]) + _dot(hi, wg_s[half:])
        up = _dot(lo, wu_s[:half]) + _dot(hi, wu_s[half:])
        o_ref[...] = (_silu(gt) * up).astype(BF16)


def gate_up(x_packed, w_gate, w_up, layer, steps, tm, tn):
    n_slots, half = x_packed.shape
    _, _, d, f = w_gate.shape
    n_steps = steps[0].shape[0]
    grid_spec = pltpu.PrefetchScalarGridSpec(
        num_scalar_prefetch=len(steps),
        grid=(n_steps,),
        in_specs=[pl.BlockSpec((tm, half), lambda s, sb, *_: (sb[s], 0)),
                  pl.BlockSpec(memory_space=pl.ANY), pl.BlockSpec(memory_space=pl.ANY)],
        out_specs=pl.BlockSpec((tm, tn), lambda s, sb, so, *_: (sb[s], so[s])),
        scratch_shapes=[pltpu.VMEM((2, d, tn), F32), pltpu.VMEM((d, tn), BF16),
                        pltpu.VMEM((d, tn), BF16), pltpu.SemaphoreType.DMA((1,))])
    return pl.pallas_call(
        functools.partial(_gate_up_kernel, layer, half, tn),
        grid_spec=grid_spec,
        out_shape=jax.ShapeDtypeStruct((n_slots, f), BF16),
        compiler_params=_cp("arbitrary"),
        name="gate_up",
    )(*steps, x_packed, w_gate, w_up)


def _down_kernel(layer, tn, sb_ref, so_ref, sf_ref, sg_ref, nv_ref, ge_ref, gj_ref, ng_ref,
                 h_ref, w_hbm, o_ref, stage, w_s, sem):
    s = pl.program_id(0)

    @pl.when(s >= nv_ref[0])
    def _():
        o_ref[...] = jnp.zeros_like(o_ref)

    def copies(g):
        col = pl.ds(pl.multiple_of(gj_ref[g] * tn, tn), tn)
        return (pltpu.make_async_copy(w_hbm.at[layer, ge_ref[g], :, col], stage, sem.at[0]),)

    def consume():
        w_s[...] = stage[...].astype(BF16)

    _stream_group_weights(s, sf_ref, sg_ref, ng_ref, copies, consume)

    @pl.when(s < nv_ref[0])
    def _():
        y = _dot(h_ref[...], w_s[...])
        hw = y.shape[1] // 2
        o_ref[...] = _pack_words(y[:, :hw], y[:, hw:])


def down(h1, w_down, layer, steps, tm, tn):
    n_slots, f = h1.shape
    d = w_down.shape[3]
    n_steps = steps[0].shape[0]
    grid_spec = pltpu.PrefetchScalarGridSpec(
        num_scalar_prefetch=len(steps),
        grid=(n_steps,),
        in_specs=[pl.BlockSpec((tm, f), lambda s, sb, *_: (sb[s], 0)),
                  pl.BlockSpec(memory_space=pl.ANY)],
        out_specs=pl.BlockSpec((tm, tn // 2), lambda s, sb, so, *_: (sb[s], so[s])),
        scratch_shapes=[pltpu.VMEM((f, tn), F32), pltpu.VMEM((f, tn), BF16),
                        pltpu.SemaphoreType.DMA((1,))])
    return pl.pallas_call(
        functools.partial(_down_kernel, layer, tn),
        grid_spec=grid_spec,
        out_shape=jax.ShapeDtypeStruct((n_slots, d // 2), U32),
        compiler_params=_cp("arbitrary"),
        name="down",
    )(*steps, h1, w_down)


def _combine_kernel(npb, tb, tn, slot_ref, y_hbm, w_ref, sh_ref, x_ref, gp_ref, gs_ref, o_ref,
                    buf, sem):
    i = pl.program_id(0)
    n = pl.num_programs(0)

    def start_block(blk, slot):
        def body(r, c):
            tok = blk * tb + r
            for k in range(TOP_K):
                pltpu.make_async_copy(y_hbm.at[pl.ds(slot_ref[tok * TOP_K + k], 1)],
                                      buf.at[slot, pl.ds(k * tb + r, 1)], sem.at[slot]).start()
            return c
        lax.fori_loop(0, tb, body, 0)

    @pl.when(i == 0)
    def _():
        start_block(0, 0)

    @pl.when(i + 1 < n)
    def _():
        start_block(i + 1, (i + 1) % 2)

    cur = i % 2
    pltpu.make_async_copy(y_hbm.at[pl.ds(0, TOP_K * tb)], buf.at[cur], sem.at[cur]).wait()

    smp = i >= npb
    gate = lambda c0, c1: jnp.where(smp, gs_ref[:, c0:c1], gp_ref[0, :, c0:c1])
    w = w_ref[...]
    sh = sh_ref
    hw = tn // 2
    cw = min(hw, COMBINE_CHUNK)
    for c0 in range(0, sh_ref.shape[1], cw):
        a = (c0 // hw) * tn + c0 % hw
        lo = _unpack_lo(sh[:, c0:c0 + cw])
        hi = _unpack_hi(sh[:, c0:c0 + cw])
        for k in range(TOP_K):
            p = buf[cur, k * tb:(k + 1) * tb, c0:c0 + cw]
            wk = w[:, k:k + 1]
            lo = lo + wk * _unpack_lo(p)
            hi = hi + wk * _unpack_hi(p)
        o_ref[:, a:a + cw] = x_ref[:, a:a + cw] + gate(a, a + cw) * lo
        o_ref[:, a + hw:a + hw + cw] = (x_ref[:, a + hw:a + hw + cw]
                                        + gate(a + hw, a + hw + cw) * hi)


def combine(y_sorted, slots_flat, wts, y_shared, x, mods, k_gate, tn, tb=64):
    t, d = x.shape
    half = d // 2
    row = lambda i, sl: i
    grid_spec = pltpu.PrefetchScalarGridSpec(
        num_scalar_prefetch=1,
        grid=(t // tb,),
        in_specs=[pl.BlockSpec(memory_space=pl.ANY),
                  pl.BlockSpec((tb, TOP_K), lambda i, sl: (i, 0)),
                  pl.BlockSpec((tb, half), lambda i, sl: (i, 0)),
                  pl.BlockSpec((tb, d), lambda i, sl: (i, 0)),
                  *mods.specs(k_gate, tb, row)],
        out_specs=pl.BlockSpec((tb, d), lambda i, sl: (i, 0)),
        scratch_shapes=[pltpu.VMEM((2, TOP_K * tb, half), U32), pltpu.SemaphoreType.DMA((2,))])
    return pl.pallas_call(
        functools.partial(_combine_kernel, mods.tp // tb, tb, tn),
        grid_spec=grid_spec,
        out_shape=jax.ShapeDtypeStruct((t, d), F32),
        compiler_params=_cp("arbitrary"),
        name="combine",
    )(slots_flat, y_sorted, wts, y_shared, x, *mods.args())


def _dispatch_tables(idx, n_exp, tm, n_tiles):
    t, k = idx.shape
    a = t * k
    flat_e = idx.reshape(a)
    onehot = flat_e[:, None] == jnp.arange(n_exp, dtype=I32)[None, :]
    csum = jnp.cumsum(onehot.astype(I32), axis=0)
    counts = csum[-1]
    nblk = (counts + tm - 1) // tm
    blk_start = jnp.cumsum(nblk) - nblk
    slot = jnp.sum(jnp.where(onehot, (blk_start * tm)[None, :] + csum - 1, 0), axis=1)
    n_blocks = a // tm + n_exp
    n_slots = n_blocks * tm
    slot_tok = jnp.zeros((n_slots,), I32).at[slot].set(jnp.arange(a, dtype=I32) // k,
                                                     unique_indices=True)
    steps = [_grouped_steps(nblk, blk_start, n_blocks, nt) for nt in n_tiles]
    return slot, slot_tok, steps, n_slots


def _lookup(table, idx):
    n = table.shape[0]
    hit = idx[:, None] == jnp.arange(n, dtype=I32)[None, :]
    return jnp.sum(jnp.where(hit, table[None, :], 0), axis=1)


def _grouped_steps(nblk, blk_start, n_blocks, n_tiles):
    n_exp = nblk.shape[0]
    n_steps = n_blocks * n_tiles
    step_end = jnp.cumsum(nblk) * n_tiles
    total = step_end[-1]
    sidx = jnp.arange(n_steps, dtype=I32)
    valid = sidx < total
    s = jnp.minimum(sidx, total - 1)
    e = jnp.sum(step_end[None, :] <= s[:, None], axis=1).astype(I32)
    nb_e = _lookup(nblk, e)
    r = s - (_lookup(step_end, e) - nb_e * n_tiles)
    j = r // nb_e
    i = r % nb_e
    spare = sidx - total
    blk = jnp.where(valid, _lookup(blk_start, e) + i, jnp.sum(nblk) + spare // n_tiles)
    j_out = jnp.where(valid, j, spare % n_tiles)
    first = jnp.where((i == 0) & valid, 1, 0)
    used = (nblk > 0).astype(I32)
    used_before = jnp.cumsum(used) - used
    grp = _lookup(used_before, e) * n_tiles + j
    n_grp = jnp.sum(used) * n_tiles
    gidx = jnp.arange(n_exp * n_tiles, dtype=I32)
    g_e = jnp.minimum(jnp.sum((used_before + used)[None, :] <= (gidx // n_tiles)[:, None], axis=1),
                      n_exp - 1)
    as_i32 = lambda a: a.astype(I32)
    return (as_i32(blk), as_i32(j_out), as_i32(first), as_i32(grp), as_i32(total.reshape(1)),
            as_i32(g_e), as_i32(gidx % n_tiles), as_i32(n_grp.reshape(1)))


def _dense_steps(n_blocks, n_tiles):
    s = jnp.arange(n_blocks * n_tiles, dtype=I32)
    j = s // n_blocks
    i = s % n_blocks
    g = jnp.arange(n_tiles, dtype=I32)
    return (i, j, (i == 0).astype(I32), j, jnp.full((1,), n_blocks * n_tiles, I32),
            jnp.zeros_like(g), g, jnp.full((1,), n_tiles, I32))


def _pack_rows(h):
    half = h.shape[1] // 2
    return lax.bitcast_convert_type(jnp.stack([h[:, :half], h[:, half:]], axis=-1), U32)


def moe_ffn(x, g, mods, layer, router_wt, router_bias, w_gate, w_up, w_down, sw_gate, sw_up,
            sw_down, tl):
    t, d = x.shape
    _, n_exp, _, f = w_gate.shape
    tm, tn_up, tn_down = tl["moe_tm"], tl["tn_up"], tl["tn_down"]
    h, idx_t, wts_t = ffn_in(x, g, mods, router_wt, router_bias, tm=tl["ffn_tm"])
    idx = idx_t.T
    wts = wts_t.T
    hp = _pack_rows(h)
    n_up, n_dn = f // tn_up, d // tn_down
    slot, slot_tok, (steps_up, steps_dn), n_slots = _dispatch_tables(idx, n_exp, tm, (n_up, n_dn))
    xs = gather_rows(hp, slot_tok, tl["gather_rows"])
    h1 = gate_up(xs, w_gate, w_up, layer, steps_up, tm, tn_up)
    ys = down(h1, w_down, layer, steps_dn, tm, tn_down)
    sh_steps_up = _dense_steps(t // tm, sw_gate.shape[3] // tn_up)
    sh_steps_dn = _dense_steps(t // tm, d // tn_down)
    s1 = gate_up(hp, sw_gate, sw_up, layer, sh_steps_up, tm, tn_up)
    ysh = down(s1, sw_down, layer, sh_steps_dn, tm, tn_down)
    return combine(ys, slot, wts, ysh, x, mods, 5, tn_down, tb=tl["comb_tb"])


def _rope_tables(pos):
    half = QK_ROPE // 2
    inv_freq = ROPE_THETA ** (-jnp.arange(half, dtype=F32) / half)
    ang = pos.astype(F32)[:, None] * inv_freq[None, :]
    z = jnp.zeros((pos.shape[0], LANES - QK_ROPE), F32)
    cos = jnp.concatenate([jnp.cos(ang), jnp.cos(ang), z], axis=1)
    sin = jnp.concatenate([jnp.sin(ang), jnp.sin(ang), z], axis=1)
    return cos, sin


def _rot_cols(w):
    half = w.shape[-1] // 2
    return jnp.concatenate([-w[..., half:], w[..., :half]], axis=-1)


def _pad_cols(w, n):
    return jnp.concatenate([w, jnp.zeros(w.shape[:-1] + (n - w.shape[-1],), w.dtype)], axis=-1)


TILES = dict(adaln_tn=512, norm_tm=256, mla_tm=256, qup_tm=512, kv_tm=512, kv_tn=1024,
             proj_tm=256, proj_tn=512, tq=1024, tk=1024, n_pg=16, pool_ts=256, pool_seqs=8,
             ffn_tm=256, gather_rows=512, moe_tm=256, tn_up=512, tn_down=4096, comb_tb=64)


def kernel(x_prompt, x_sample, cache_kv_latent, cache_k_rope, state_pool, page_table, c_prompt,
           c_sample, ada_w, ada_b, norm_mix_g, norm_ffn_g, mla_w_in, mla_q_norm_g, mla_w_uq,
           mla_kv_norm_g, mla_w_uk, mla_w_uv, mla_w_o, pool_w, pool_scale, router_w, router_bias,
           exp_w_gate, exp_w_up, exp_w_down, shared_w_gate, shared_w_up, shared_w_down,
           final_norm_g):
    return _step(TILES, x_prompt, x_sample, cache_kv_latent, cache_k_rope, state_pool, page_table,
                 c_prompt, c_sample, ada_w, ada_b, norm_mix_g, norm_ffn_g, mla_w_in, mla_q_norm_g,
                 mla_w_uq, mla_kv_norm_g, mla_w_uk, mla_w_uv, mla_w_o, pool_w, pool_scale,
                 router_w, router_bias, exp_w_gate, exp_w_up, exp_w_down, shared_w_gate,
                 shared_w_up, shared_w_down, final_norm_g)


def _step(tl, x_prompt, x_sample, cache_kv_latent, cache_k_rope, state_pool, page_table, c_prompt,
          c_sample, ada_w, ada_b, norm_mix_g, norm_ffn_g, mla_w_in, mla_q_norm_g, mla_w_uq,
          mla_kv_norm_g, mla_w_uk, mla_w_uv, mla_w_o, pool_w, pool_scale, router_w, router_bias,
          exp_w_gate, exp_w_up, exp_w_down, shared_w_gate, shared_w_up, shared_w_down,
          final_norm_g):
    n_p, seq, d = x_prompt.shape
    n_s, dec_seq, _ = x_sample.shape
    depth = ada_w.shape[0]
    tp, ts_tok = n_p * seq, n_s * dec_seq
    t = tp + ts_tok
    page = cache_kv_latent.shape[2]
    past = page_table.shape[1] * page
    q_lora = mla_q_norm_g.shape[1]
    kv_lora = mla_kv_norm_g.shape[1]
    n_h = N_HEADS
    sm_scale = (QK_NOPE + QK_ROPE) ** -0.5
    n_pool_hist = max(POOL_WINDOWS) - 1

    x = jnp.concatenate([x_prompt.reshape(tp, d), x_sample.reshape(ts_tok, d)], axis=0)

    n_c = n_p + n_s
    n_c_pad = -(-n_c // 8) * 8
    c_all = jnp.concatenate([c_prompt, c_sample, jnp.zeros((n_c_pad - n_c, d), F32)], axis=0)
    mod_all = adaln(c_all, ada_w, ada_b, tl["adaln_tn"])

    pos = jnp.concatenate([jnp.tile(jnp.arange(seq), n_p),
                           jnp.tile(past + jnp.arange(dec_seq), n_s)])
    cos_t, sin_t = _rope_tables(pos)

    lat_p, rope_p, pool_p, lat_s, rope_s, pool_s = [], [], [], [], [], []
    for i in range(depth):
        j = i // 2
        mods = Mods(mod_all[i, :n_p].reshape(n_p, 1, 6 * d),
                    jnp.repeat(mod_all[i, n_p:n_c], dec_seq, axis=0), seq, tp, d)
        if i % 2 == 0:
            w_in = mla_w_in[j]
            w_r = w_in[:, q_lora + kv_lora:]
            w_aug = jnp.concatenate([w_in[:, :q_lora + kv_lora], _pad_cols(w_r, LANES),
                                     _pad_cols(_rot_cols(w_r), LANES)], axis=1).astype(BF16)
            cq, ckv, ckv_b, kr, kr_b = mla_in(x, norm_mix_g[i], mods, w_aug, mla_q_norm_g[j],
                                              mla_kv_norm_g[j], cos_t, sin_t, q_lora, kv_lora,
                                              tm=tl["mla_tm"])
            wq = mla_w_uq[j].reshape(q_lora, n_h, QK_NOPE + QK_ROPE)
            wq_r = wq[:, :, QK_NOPE:]
            w1 = jnp.concatenate([wq, jnp.zeros((q_lora, n_h, HEAD_PAD - QK_NOPE - QK_ROPE), F32)],
                                 axis=2).reshape(q_lora, n_h * HEAD_PAD).astype(BF16)
            w2 = _pad_cols(_rot_cols(wq_r), LANES).reshape(q_lora, n_h * LANES).astype(BF16)
            q_cat = q_up(cq, w1, w2, cos_t, sin_t, tm=tl["qup_tm"])

            w_kv = jnp.concatenate([mla_w_uk[j], mla_w_uv[j]], axis=1).astype(BF16)
            kv = matmul(ckv_b[:tp], w_kv, BF16, tl["kv_tm"], tl["kv_tn"])
            attn_p = prompt_attention(q_cat, kv, kr_b, n_p, seq, sm_scale, tl["tq"], tl["tk"])

            w_uk_t = mla_w_uk[j].reshape(kv_lora, n_h, QK_NOPE).transpose(1, 2, 0).astype(BF16)
            w_uv_h = mla_w_uv[j].reshape(kv_lora, n_h, V_HEAD).transpose(1, 0, 2).astype(BF16)
            q_s = q_cat[tp:].reshape(ts_tok, n_h, HEAD_PAD)
            q_nope_s = q_s[:, :, :QK_NOPE].transpose(1, 0, 2)
            q_lat = heads_matmul(
                q_nope_s, w_uk_t,
                pl.BlockSpec((1, ts_tok, QK_NOPE), lambda h: (h, 0, 0)),
                pl.BlockSpec((1, ts_tok, kv_lora), lambda h: (h, 0, 0)),
                jax.ShapeDtypeStruct((n_h, ts_tok, kv_lora), BF16))
            q_lat = q_lat.transpose(1, 0, 2).reshape(n_s, dec_seq * n_h, kv_lora)
            q_rope_s = q_s[:, :, QK_NOPE:QK_NOPE + QK_ROPE].reshape(n_s, dec_seq * n_h, QK_ROPE)
            kpad = page - dec_seq
            new_lat = jnp.pad(ckv_b[tp:].reshape(n_s, dec_seq, kv_lora), ((0, 0), (0, kpad), (0, 0)))
            new_rope = jnp.pad(kr_b[tp:, :QK_ROPE].reshape(n_s, dec_seq, QK_ROPE),
                               ((0, 0), (0, kpad), (0, 0))).swapaxes(1, 2)
            o_lat = sample_attention(q_lat, q_rope_s, cache_kv_latent,
                                     cache_k_rope.swapaxes(2, 3), j,
                                     page_table, new_lat, new_rope, dec_seq, sm_scale,
                                     n_pg=tl["n_pg"])
            o_lat = o_lat.reshape(ts_tok, n_h, kv_lora).transpose(1, 0, 2)
            attn_s = heads_matmul(
                o_lat, w_uv_h,
                pl.BlockSpec((1, ts_tok, kv_lora), lambda h: (h, 0, 0)),
                pl.BlockSpec((ts_tok, V_HEAD), lambda h: (0, h)),
                jax.ShapeDtypeStruct((ts_tok, n_h * V_HEAD), BF16))
            attn = jnp.concatenate([attn_p, attn_s], axis=0)
            x = proj_residual(attn, mla_w_o[j].astype(BF16), x, mods, 2, tl["proj_tm"],
                              tl["proj_tn"])

            lat_p.append(ckv[:tp].reshape(n_p, seq, kv_lora))
            rope_p.append(kr[:tp, :QK_ROPE].reshape(n_p, seq, QK_ROPE))
            lat_s.append(ckv[tp:].reshape(n_s, dec_seq, kv_lora))
            rope_s.append(kr[tp:, :QK_ROPE].reshape(n_s, dec_seq, QK_ROPE))
        else:
            h = modnorm(x, norm_mix_g[i], mods, 0, 1, F32, tm=tl["norm_tm"])
            pw = pool_w[j].astype(BF16)
            hp3 = h[:tp].reshape(n_p, seq, d)
            gate_p = mods.mp[:, :, 2 * d:3 * d]
            xp_new = pool_mixer(hp3, x[:tp].reshape(n_p, seq, d), gate_p, pw, pool_scale[j], 0,
                                tl["pool_ts"])
            hs3 = h[tp:].reshape(n_s, dec_seq, d)
            rows = POOL_HALO + 8
            tail = rows - POOL_HALO - dec_seq
            lead = POOL_HALO - n_pool_hist
            hh = jnp.concatenate([jnp.zeros((n_s, lead, d), F32), state_pool[j], hs3,
                                  jnp.zeros((n_s, tail, d), F32)], axis=1)
            pad3 = lambda a: jnp.pad(a, ((0, 0), (POOL_HALO, tail), (0, 0)))
            xs3 = x[tp:].reshape(n_s, dec_seq, d)
            gate_s = mods.ms[:, 2 * d:3 * d].reshape(n_s, dec_seq, d)
            seqs = tl["pool_seqs"]
            xs_new = pool_mixer(hh.reshape(n_s // seqs, seqs * rows, d),
                                pad3(xs3).reshape(n_s // seqs, seqs * rows, d),
                                pad3(gate_s).reshape(n_s // seqs, seqs * rows, d),
                                pw, pool_scale[j], past - POOL_HALO, seqs * rows, period=rows)
            xs_new = xs_new.reshape(n_s, rows, d)[:, POOL_HALO:POOL_HALO + dec_seq]
            x = jnp.concatenate([xp_new.reshape(tp, d), xs_new.reshape(ts_tok, d)], axis=0)
            pool_p.append(hp3[:, seq - n_pool_hist:])
            pool_s.append(jnp.concatenate([state_pool[j], hs3], axis=1)[:, dec_seq:])

        x = moe_ffn(x, norm_ffn_g[i], mods, i, router_w[i].T, router_bias[i], exp_w_gate,
                    exp_w_up, exp_w_down, shared_w_gate[:, None], shared_w_up[:, None],
                    shared_w_down[:, None], tl)

    y_p, y_s = rmsnorm_split(x, final_norm_g, tp, tm=tl["norm_tm"])
    return (y_p.reshape(n_p, seq, d), y_s.reshape(n_s, dec_seq, d),
            jnp.stack(lat_p), jnp.stack(rope_p), jnp.stack(pool_p),
            jnp.stack(lat_s), jnp.stack(rope_s), jnp.stack(pool_s))
```

```python
import functools

import jax
import jax.numpy as jnp
from jax import lax
from jax.experimental import pallas as pl
from jax.experimental.pallas import tpu as pltpu

F32 = jnp.float32
BF16 = jnp.bfloat16
U32 = jnp.uint32
I32 = jnp.int32

EPS = 1e-6
ROPE_THETA = 10000.0
ROUTED_SCALE = 2.5
N_GROUPS = 8
TOPK_GROUPS = 4
TOP_K = 8
POOL_WINDOWS = (2, 4, 8, 16)
POOL_HALO = 16
N_HEADS = 32
QK_NOPE = 128
QK_ROPE = 64
V_HEAD = 128
LANES = 128
HEAD_PAD = 2 * LANES
VMEM_LIMIT = 56 * 1024 * 1024
COMBINE_CHUNK = 256
NEG_BIG = -1e30
LOG2_E = 1.4426950408889634


def _cp(*sem, vmem=VMEM_LIMIT):
    return pltpu.CompilerParams(dimension_semantics=sem, vmem_limit_bytes=vmem)


def _dot(a, b):
    return jnp.dot(a, b, preferred_element_type=F32)


def _dot_nt(a, b):
    return lax.dot_general(a, b, (((1,), (1,)), ((), ())), preferred_element_type=F32)


def _rms(x, g):
    return x * lax.rsqrt(jnp.mean(x * x, axis=-1, keepdims=True) + EPS) * g


def _silu(x):
    return x * jax.nn.sigmoid(x)


def _unpack_lo(p):
    return lax.bitcast_convert_type(p << 16, F32)


def _unpack_hi(p):
    return lax.bitcast_convert_type(p & jnp.uint32(0xFFFF0000), F32)


def _pack_words(lo, hi):
    lo_b = lax.bitcast_convert_type(lo.astype(BF16).astype(F32), U32)
    hi_b = lax.bitcast_convert_type(hi.astype(BF16).astype(F32), U32)
    return (hi_b & jnp.uint32(0xFFFF0000)) | (lo_b >> 16)


def _adaln_kernel(c_ref, w_ref, b_ref, o_ref):
    s = _silu(c_ref[...]).astype(BF16)
    o_ref[0] = _dot(s, w_ref[0].astype(BF16)) + b_ref[0]


def adaln(c_all, ada_w, ada_b, tn=512):
    depth, d, n = ada_w.shape
    r = c_all.shape[0]
    return pl.pallas_call(
        _adaln_kernel,
        grid=(depth, n // tn),
        in_specs=[pl.BlockSpec((r, d), lambda l, j: (0, 0)),
                  pl.BlockSpec((1, d, tn), lambda l, j: (l, 0, j)),
                  pl.BlockSpec((1, 1, tn), lambda l, j: (l, 0, j))],
        out_specs=pl.BlockSpec((1, r, tn), lambda l, j: (l, 0, j)),
        out_shape=jax.ShapeDtypeStruct((depth, r, n), F32),
        compiler_params=_cp("arbitrary", "arbitrary"),
        name="adaln",
    )(c_all, ada_w, ada_b.reshape(depth, 1, n))


class Mods:
    def __init__(self, mp, ms, seq, n_prompt_tok, d):
        self.mp, self.ms, self.seq, self.tp, self.d = mp, ms, seq, n_prompt_tok, d

    def specs(self, k, tm, row_of):
        npb = self.tp // tm
        per_seq = self.seq // tm
        n_p = self.mp.shape[0]

        def p_map(*g):
            return (jnp.minimum(row_of(*g) // per_seq, n_p - 1), 0, k)

        def s_map(*g):
            return (jnp.maximum(row_of(*g) - npb, 0), k)

        return [pl.BlockSpec((1, 1, self.d), p_map), pl.BlockSpec((tm, self.d), s_map)]

    def args(self):
        return [self.mp, self.ms]


def _pick(is_sample, p_ref, s_ref):
    return jnp.where(is_sample, s_ref[...], p_ref[0])


def _modnorm_kernel(npb, x_ref, g_ref, shp_ref, shs_ref, scp_ref, scs_ref, o_ref):
    smp = pl.program_id(0) >= npb
    shift = _pick(smp, shp_ref, shs_ref)
    scale = _pick(smp, scp_ref, scs_ref)
    o_ref[...] = (_rms(x_ref[...], g_ref[...]) * (1.0 + scale) + shift).astype(o_ref.dtype)


def modnorm(x, g, mods, k_shift, k_scale, out_dtype, tm=256):
    t, d = x.shape
    row = lambda i: i
    return pl.pallas_call(
        functools.partial(_modnorm_kernel, mods.tp // tm),
        grid=(t // tm,),
        in_specs=[pl.BlockSpec((tm, d), lambda i: (i, 0)),
                  pl.BlockSpec((1, d), lambda i: (0, 0)),
                  *mods.specs(k_shift, tm, row), *mods.specs(k_scale, tm, row)],
        out_specs=pl.BlockSpec((tm, d), lambda i: (i, 0)),
        out_shape=jax.ShapeDtypeStruct((t, d), out_dtype),
        compiler_params=_cp("arbitrary"),
        name="modnorm",
    )(x, g.reshape(1, d), *mods.args(), *mods.args())


def _rmsnorm_kernel(npb, x_ref, g_ref, op_ref, os_ref):
    i = pl.program_id(0)
    y = _rms(x_ref[...], g_ref[...])

    @pl.when(i < npb)
    def _():
        op_ref[...] = y

    @pl.when(i >= npb)
    def _():
        os_ref[...] = y


def rmsnorm_split(x, g, tp, tm=256):
    t, d = x.shape
    npb = tp // tm
    return pl.pallas_call(
        functools.partial(_rmsnorm_kernel, npb),
        grid=(t // tm,),
        in_specs=[pl.BlockSpec((tm, d), lambda i: (i, 0)), pl.BlockSpec((1, d), lambda i: (0, 0))],
        out_specs=[pl.BlockSpec((tm, d), lambda i: (jnp.minimum(i, npb - 1), 0)),
                   pl.BlockSpec((tm, d), lambda i: (jnp.maximum(i - npb, 0), 0))],
        out_shape=[jax.ShapeDtypeStruct((tp, d), F32), jax.ShapeDtypeStruct((t - tp, d), F32)],
        compiler_params=_cp("arbitrary"),
        name="final_rmsnorm",
    )(x, g.reshape(1, d))


def _mla_in_kernel(npb, q_lora, kv_lora, x_ref, g_ref, shp_ref, shs_ref, scp_ref, scs_ref,
                   w_ref, gq_ref, gkv_ref, cos_ref, sin_ref,
                   cq_ref, ckv_ref, ckvb_ref, kr_ref, krb_ref):
    smp = pl.program_id(0) >= npb
    shift = _pick(smp, shp_ref, shs_ref)
    scale = _pick(smp, scp_ref, scs_ref)
    h = (_rms(x_ref[...], g_ref[...]) * (1.0 + scale) + shift).astype(BF16)
    a = _dot(h, w_ref[...])
    cq_ref[...] = _rms(a[:, :q_lora], gq_ref[...]).astype(BF16)
    ckv = _rms(a[:, q_lora:q_lora + kv_lora], gkv_ref[...])
    ckv_ref[...] = ckv
    ckvb_ref[...] = ckv.astype(BF16)
    o = q_lora + kv_lora
    kr = a[:, o:o + LANES] * cos_ref[...] + a[:, o + LANES:o + 2 * LANES] * sin_ref[...]
    kr_ref[...] = kr
    krb_ref[...] = kr.astype(BF16)


def mla_in(x, g, mods, w_aug, gq, gkv, cos_t, sin_t, q_lora, kv_lora, tm=256):
    t, d = x.shape
    n_aug = w_aug.shape[1]
    row = lambda i: i
    blk = lambda w: pl.BlockSpec((tm, w), lambda i: (i, 0))
    return pl.pallas_call(
        functools.partial(_mla_in_kernel, mods.tp // tm, q_lora, kv_lora),
        grid=(t // tm,),
        in_specs=[blk(d), pl.BlockSpec((1, d), lambda i: (0, 0)),
                  *mods.specs(0, tm, row), *mods.specs(1, tm, row),
                  pl.BlockSpec((d, n_aug), lambda i: (0, 0)),
                  pl.BlockSpec((1, q_lora), lambda i: (0, 0)),
                  pl.BlockSpec((1, kv_lora), lambda i: (0, 0)),
                  blk(LANES), blk(LANES)],
        out_specs=[blk(q_lora), blk(kv_lora), blk(kv_lora), blk(LANES), blk(LANES)],
        out_shape=[jax.ShapeDtypeStruct((t, q_lora), BF16),
                   jax.ShapeDtypeStruct((t, kv_lora), F32),
                   jax.ShapeDtypeStruct((t, kv_lora), BF16),
                   jax.ShapeDtypeStruct((t, LANES), F32),
                   jax.ShapeDtypeStruct((t, LANES), BF16)],
        compiler_params=_cp("arbitrary"),
        name="mla_in",
    )(x, g.reshape(1, d), *mods.args(), *mods.args(), w_aug, gq.reshape(1, -1),
      gkv.reshape(1, -1), cos_t, sin_t)


def _q_up_kernel(hg, cq_ref, w1_ref, w2_ref, cos_ref, sin_ref, o_ref):
    cq = cq_ref[...]
    a = _dot(cq, w1_ref[...])
    r = _dot(cq, w2_ref[...])
    c = cos_ref[...]
    s = sin_ref[...]
    for h in range(hg):
        b = h * HEAD_PAD
        o_ref[:, b:b + LANES] = a[:, b:b + LANES].astype(BF16)
        o_ref[:, b + LANES:b + HEAD_PAD] = (
            a[:, b + LANES:b + HEAD_PAD] * c + r[:, h * LANES:(h + 1) * LANES] * s).astype(BF16)


def q_up(cq, w1, w2, cos_t, sin_t, tm=512, hg=4):
    t, k = cq.shape
    n_h = w1.shape[1] // HEAD_PAD
    return pl.pallas_call(
        functools.partial(_q_up_kernel, hg),
        grid=(n_h // hg, t // tm),
        in_specs=[pl.BlockSpec((tm, k), lambda j, i: (i, 0)),
                  pl.BlockSpec((k, hg * HEAD_PAD), lambda j, i: (0, j)),
                  pl.BlockSpec((k, hg * LANES), lambda j, i: (0, j)),
                  pl.BlockSpec((tm, LANES), lambda j, i: (i, 0)),
                  pl.BlockSpec((tm, LANES), lambda j, i: (i, 0))],
        out_specs=pl.BlockSpec((tm, hg * HEAD_PAD), lambda j, i: (i, j)),
        out_shape=jax.ShapeDtypeStruct((t, n_h * HEAD_PAD), BF16),
        compiler_params=_cp("arbitrary", "arbitrary"),
        name="q_up",
    )(cq, w1, w2, cos_t, sin_t)


def _mm_kernel(a_ref, b_ref, o_ref):
    o_ref[...] = _dot(a_ref[...], b_ref[...]).astype(o_ref.dtype)


def matmul(a, b, out_dtype, tm=512, tn=1024):
    m, k = a.shape
    n = b.shape[1]
    tm, tn = min(tm, m), min(tn, n)
    return pl.pallas_call(
        _mm_kernel,
        grid=(n // tn, m // tm),
        in_specs=[pl.BlockSpec((tm, k), lambda j, i: (i, 0)),
                  pl.BlockSpec((k, tn), lambda j, i: (0, j))],
        out_specs=pl.BlockSpec((tm, tn), lambda j, i: (i, j)),
        out_shape=jax.ShapeDtypeStruct((m, n), out_dtype),
        compiler_params=_cp("arbitrary", "arbitrary"),
        name="matmul",
    )(a, b)


def _proj_res_kernel(npb, a_ref, w_ref, x_ref, gp_ref, gs_ref, o_ref):
    smp = pl.program_id(1) >= npb
    gate = _pick(smp, gp_ref, gs_ref)
    o_ref[...] = x_ref[...] + gate * _dot(a_ref[...], w_ref[...])


def proj_residual(a, w, x, mods, k_gate, tm=256, tn=512):
    t, k = a.shape
    d = w.shape[1]
    npb = mods.tp // tm
    per_seq = mods.seq // tm
    n_p = mods.mp.shape[0]
    cb = mods.d // tn
    gp = pl.BlockSpec((1, 1, tn), lambda j, i: (jnp.minimum(i // per_seq, n_p - 1), 0, k_gate * cb + j))
    gs = pl.BlockSpec((tm, tn), lambda j, i: (jnp.maximum(i - npb, 0), k_gate * cb + j))
    return pl.pallas_call(
        functools.partial(_proj_res_kernel, npb),
        grid=(d // tn, t // tm),
        in_specs=[pl.BlockSpec((tm, k), lambda j, i: (i, 0)),
                  pl.BlockSpec((k, tn), lambda j, i: (0, j)),
                  pl.BlockSpec((tm, tn), lambda j, i: (i, j)), gp, gs],
        out_specs=pl.BlockSpec((tm, tn), lambda j, i: (i, j)),
        out_shape=jax.ShapeDtypeStruct((t, d), F32),
        compiler_params=_cp("arbitrary", "arbitrary"),
        name="proj_residual",
    )(a, w, x, *mods.args())


def _flash_kernel(tq, tk, scale, q_ref, kn_ref, kr_ref, v_ref, o_ref, m_s, l_s, acc_s):
    qi = pl.program_id(2)
    ki = pl.program_id(3)
    nk = pl.num_programs(3)

    @pl.when(ki == 0)
    def _():
        m_s[...] = jnp.full_like(m_s, NEG_BIG)
        l_s[...] = jnp.zeros_like(l_s)
        acc_s[...] = jnp.zeros_like(acc_s)

    def update(masked):
        k = jnp.concatenate([kn_ref[...], kr_ref[...]], axis=1)
        s = _dot_nt(q_ref[...], k) * (scale * LOG2_E)
        if masked:
            qpos = qi * tq + lax.broadcasted_iota(I32, (tq, tk), 0)
            kpos = ki * tk + lax.broadcasted_iota(I32, (tq, tk), 1)
            s = jnp.where(kpos <= qpos, s, NEG_BIG)
        m_old = m_s[...]
        m_new = jnp.maximum(m_old, jnp.max(s, axis=-1, keepdims=True))
        alpha = jnp.exp2(m_old - m_new)
        p = jnp.exp2(s - m_new)
        l_s[...] = alpha * l_s[...] + jnp.sum(p, axis=-1, keepdims=True)
        acc_s[...] = alpha * acc_s[...] + _dot(p.astype(BF16), v_ref[...])
        m_s[...] = m_new

    live = ki * tk <= qi * tq + (tq - 1)
    crosses = ki * tk + (tk - 1) > qi * tq

    @pl.when(jnp.logical_and(live, crosses))
    def _():
        update(True)

    @pl.when(jnp.logical_and(live, jnp.logical_not(crosses)))
    def _():
        update(False)

    @pl.when(ki == nk - 1)
    def _():
        o_ref[...] = (acc_s[...] / l_s[...]).astype(o_ref.dtype)


def prompt_attention(q_cat, kv, kr_b, n_seq, seq, scale, tq=1024, tk=1024):
    n_h = q_cat.shape[1] // HEAD_PAD
    tq, tk = min(tq, seq), min(tk, seq)
    nq, nk = seq // tq, seq // tk

    def kmap(b, h, qi, ki):
        return jnp.minimum(ki, (qi * tq + tq - 1) // tk)

    return pl.pallas_call(
        functools.partial(_flash_kernel, tq, tk, scale),
        grid=(n_seq, n_h, nq, nk),
        in_specs=[pl.BlockSpec((tq, HEAD_PAD), lambda b, h, qi, ki: (b * nq + qi, h)),
                  pl.BlockSpec((tk, QK_NOPE), lambda b, h, qi, ki: (b * nk + kmap(b, h, qi, ki), h)),
                  pl.BlockSpec((tk, LANES), lambda b, h, qi, ki: (b * nk + kmap(b, h, qi, ki), 0)),
                  pl.BlockSpec((tk, V_HEAD), lambda b, h, qi, ki: (b * nk + kmap(b, h, qi, ki), n_h + h))],
        out_specs=pl.BlockSpec((tq, V_HEAD), lambda b, h, qi, ki: (b * nq + qi, h)),
        out_shape=jax.ShapeDtypeStruct((n_seq * seq, n_h * V_HEAD), BF16),
        scratch_shapes=[pltpu.VMEM((tq, 1), F32), pltpu.VMEM((tq, 1), F32),
                        pltpu.VMEM((tq, V_HEAD), F32)],
        compiler_params=_cp("arbitrary", "arbitrary", "arbitrary", "arbitrary"),
        name="prompt_attention",
    )(q_cat, kv, kr_b, kv)


def _bmm_kernel(a_ref, b_ref, o_ref):
    a = a_ref[...]
    a = a.reshape(a.shape[-2:])
    o = _dot(a, b_ref[0]).astype(o_ref.dtype)
    o_ref[...] = o.reshape(o_ref.shape)


def heads_matmul(a, b, a_spec, out_spec, out_shape):
    n_h = b.shape[0]
    return pl.pallas_call(
        _bmm_kernel,
        grid=(n_h,),
        in_specs=[a_spec, pl.BlockSpec((1,) + b.shape[1:], lambda h: (h, 0, 0))],
        out_specs=out_spec,
        out_shape=out_shape,
        compiler_params=_cp("arbitrary"),
        name="heads_matmul",
    )(a, b)


def _decode_kernel(layer, n_pg, n_grp, dec_seq, scale, pt_ref, ql_ref, qr_ref, lat_hbm, rope_hbm,
                   nlat_ref, nrope_ref, o_ref, lat_buf, rope_buf, sem, m_s, l_s, acc_s):
    i = pl.program_id(0)
    g = pl.program_id(1)
    n_seq = pl.num_programs(0)
    page = lat_hbm.shape[2]

    def copies(seq, grp, slot):
        out = []
        for k in range(n_pg):
            pg = pt_ref[seq, grp * n_pg + k]
            out.append(pltpu.make_async_copy(lat_hbm.at[layer, pg],
                                             lat_buf.at[slot, pl.ds(k * page, page)], sem.at[slot]))
            out.append(pltpu.make_async_copy(rope_hbm.at[layer, pg],
                                             rope_buf.at[slot, :, pl.ds(k * page, page)],
                                             sem.at[slot]))
        return out

    step = i * n_grp + g
    slot = step % 2

    @pl.when(jnp.logical_and(i == 0, g == 0))
    def _():
        for c in copies(0, 0, 0):
            c.start()

    @pl.when(jnp.logical_and(g < n_grp, step + 1 < n_seq * n_grp))
    def _():
        nxt = step + 1
        for c in copies(nxt // n_grp, nxt % n_grp, nxt % 2):
            c.start()

    @pl.when(g == 0)
    def _():
        m_s[...] = jnp.full_like(m_s, NEG_BIG)
        l_s[...] = jnp.zeros_like(l_s)
        acc_s[...] = jnp.zeros_like(acc_s)

    def update(lat, rope_t, mask):
        s = (_dot_nt(ql_ref[0], lat) + _dot(qr_ref[0], rope_t)) * (scale * LOG2_E)
        if mask is not None:
            s = jnp.where(mask, s, NEG_BIG)
        m_old = m_s[...]
        m_new = jnp.maximum(m_old, jnp.max(s, axis=-1, keepdims=True))
        alpha = jnp.exp2(m_old - m_new)
        p = jnp.exp2(s - m_new)
        l_s[...] = alpha * l_s[...] + jnp.sum(p, axis=-1, keepdims=True)
        acc_s[...] = alpha * acc_s[...] + _dot(p.astype(BF16), lat)
        m_s[...] = m_new

    @pl.when(g < n_grp)
    def _():
        for c in copies(i, g, slot):
            c.wait()
        update(lat_buf[slot].astype(BF16), rope_buf[slot].astype(BF16), None)

    @pl.when(g == n_grp)
    def _():
        lat = nlat_ref[0]
        rows, keys = ql_ref.shape[1], lat.shape[0]
        q_tok = lax.broadcasted_iota(I32, (rows, keys), 0) // (rows // dec_seq)
        k_tok = lax.broadcasted_iota(I32, (rows, keys), 1)
        update(lat, nrope_ref[0], k_tok <= q_tok)
        o_ref[0] = (acc_s[...] / l_s[...]).astype(o_ref.dtype)


def sample_attention(q_lat, q_rope, cache_lat, cache_rope, layer, page_table, new_lat, new_rope,
                     dec_seq, scale, n_pg=16):
    n, r, c = q_lat.shape
    page = cache_lat.shape[2]
    rd = cache_rope.shape[2]
    n_pages = page_table.shape[1]
    n_pg = min(n_pg, n_pages)
    n_grp = n_pages // n_pg
    kn = new_lat.shape[1]

    seq_map = lambda i, g, pt: (i, 0, 0)
    in_specs = [pl.BlockSpec((1, r, c), seq_map), pl.BlockSpec((1, r, rd), seq_map),
                pl.BlockSpec(memory_space=pl.ANY), pl.BlockSpec(memory_space=pl.ANY),
                pl.BlockSpec((1, kn, c), seq_map), pl.BlockSpec((1, rd, kn), seq_map)]
    grid_spec = pltpu.PrefetchScalarGridSpec(
        num_scalar_prefetch=1,
        grid=(n, n_grp + 1),
        in_specs=in_specs,
        out_specs=pl.BlockSpec((1, r, c), seq_map),
        scratch_shapes=[pltpu.VMEM((2, n_pg * page, c), F32), pltpu.VMEM((2, rd, n_pg * page), F32),
                        pltpu.SemaphoreType.DMA((2,)),
                        pltpu.VMEM((r, 1), F32), pltpu.VMEM((r, 1), F32), pltpu.VMEM((r, c), F32)])
    return pl.pallas_call(
        functools.partial(_decode_kernel, layer, n_pg, n_grp, dec_seq, scale),
        grid_spec=grid_spec,
        out_shape=jax.ShapeDtypeStruct((n, r, c), BF16),
        compiler_params=_cp("arbitrary", "arbitrary"),
        name="sample_attention",
    )(page_table, q_lat, q_rope, cache_lat, cache_rope, new_lat, new_rope)


def _pool_kernel(ts, grp_w, first_pos, period, gate_per_seq, h_ref, halo_ref, x_ref, gate_ref, w_ref,
                 ps_ref, o_ref):
    i = pl.program_id(1)
    keep = jnp.where(i > 0, 1.0, 0.0).astype(F32)
    u_all = jnp.concatenate([halo_ref[...] * keep, h_ref[...]], axis=0)
    rows = POOL_HALO + ts
    r = i * ts - POOL_HALO + lax.broadcasted_iota(I32, (rows, 1), 0)
    if period:
        r = (r + period) % period
    pos = first_pos + r
    for g, win in enumerate(POOL_WINDOWS):
        u = u_all[:, g * grp_w:(g + 1) * grp_w]
        s = u
        k = 1
        while k < win:
            s = s + pltpu.roll(s, k, axis=0)
            k *= 2
        count = jnp.clip(pos + 1, 1, win).astype(F32)
        dlt = (s / count - u)[POOL_HALO:].astype(BF16)
        cols = slice(g * grp_w, (g + 1) * grp_w)
        y = _dot(dlt, w_ref[g]) * ps_ref[:, cols]
        gate = gate_ref[0, :, cols] if gate_per_seq else gate_ref[:, cols]
        o_ref[:, cols] = x_ref[:, cols] + gate * y


def pool_mixer(h, x, gate, pool_w, pool_scale, n_seq, seq_len, first_pos, ts, period=0,
               in_place=False):
    d = h.shape[1]
    grp_w = d // len(POOL_WINDOWS)
    nb = seq_len // ts
    hb = ts // POOL_HALO
    row_map = lambda b, i: (b * nb + i, 0)
    gate_per_seq = gate.ndim == 3
    gate_spec = (pl.BlockSpec((1, 1, d), lambda b, i: (b, 0, 0)) if gate_per_seq
                 else pl.BlockSpec((ts, d), row_map))
    return pl.pallas_call(
        functools.partial(_pool_kernel, ts, grp_w, first_pos, period, gate_per_seq),
        grid=(n_seq, nb),
        in_specs=[pl.BlockSpec((ts, d), row_map),
                  pl.BlockSpec((POOL_HALO, d), lambda b, i: (jnp.maximum((b * nb + i) * hb - 1, 0), 0)),
                  pl.BlockSpec((ts, d), row_map),
                  gate_spec,
                  pl.BlockSpec(pool_w.shape, lambda b, i: (0, 0, 0)),
                  pl.BlockSpec((1, d), lambda b, i: (0, 0))],
        out_specs=pl.BlockSpec((ts, d), row_map),
        out_shape=jax.ShapeDtypeStruct(x.shape, F32),
        input_output_aliases={2: 0} if in_place else {},
        compiler_params=_cp("arbitrary", "arbitrary"),
        name="pool_mixer",
    )(h, h, x, gate, pool_w, pool_scale.reshape(1, d))


def _ffn_in_kernel(npb, n_exp, x_ref, g_ref, shp_ref, shs_ref, scp_ref, scs_ref, rw_ref, rb_ref,
                   h_ref, idx_ref, wts_ref):
    smp = pl.program_id(0) >= npb
    shift = _pick(smp, shp_ref, shs_ref)
    scale = _pick(smp, scp_ref, scs_ref)
    h = _rms(x_ref[...], g_ref[...]) * (1.0 + scale) + shift
    half = h.shape[1] // 2
    h_ref[...] = _pack_words(h[:, :half], h[:, half:])
    logits = lax.dot_general(rw_ref[...], h, (((1,), (1,)), ((), ())),
                             preferred_element_type=F32, precision=lax.Precision.HIGHEST)
    scores = jax.nn.sigmoid(logits)
    sel = scores + rb_ref[...]
    tm = sel.shape[1]
    gsz = n_exp // N_GROUPS
    eid = lax.broadcasted_iota(I32, (n_exp, tm), 0)
    gid_of_e = eid // gsz
    gs = []
    for g in range(N_GROUPS):
        blk = sel[g * gsz:(g + 1) * gsz]
        rid = lax.broadcasted_iota(I32, (gsz, tm), 0)
        m1 = jnp.max(blk, axis=0, keepdims=True)
        a1 = jnp.min(jnp.where(blk == m1, rid, gsz), axis=0, keepdims=True)
        m2 = jnp.max(jnp.where(rid == a1, -jnp.inf, blk), axis=0, keepdims=True)
        gs.append(m1 + m2)
    gs = jnp.concatenate(gs, axis=0)
    grow = lax.broadcasted_iota(I32, (N_GROUPS, tm), 0)
    gm_f = jnp.zeros((N_GROUPS, tm), F32)
    work = gs
    for _ in range(TOPK_GROUPS):
        m = jnp.max(work, axis=0, keepdims=True)
        a = jnp.min(jnp.where(work == m, grow, N_GROUPS), axis=0, keepdims=True)
        hit = grow == a
        gm_f = jnp.where(hit, 1.0, gm_f)
        work = jnp.where(hit, -jnp.inf, work)
    emask = jnp.zeros((n_exp, tm), F32)
    for g in range(N_GROUPS):
        emask = jnp.where(gid_of_e == g, gm_f[g:g + 1], emask)
    work = jnp.where(emask > 0.5, sel, -jnp.inf)
    ids, ws = [], []
    for _ in range(TOP_K):
        m = jnp.max(work, axis=0, keepdims=True)
        a = jnp.min(jnp.where(work == m, eid, n_exp), axis=0, keepdims=True)
        hit = eid == a
        ids.append(a)
        ws.append(jnp.sum(jnp.where(hit, scores, 0.0), axis=0, keepdims=True))
        work = jnp.where(hit, -jnp.inf, work)
    ids = jnp.concatenate(ids, axis=0)
    ws = jnp.concatenate(ws, axis=0)
    idx_ref[...] = ids
    wts_ref[...] = ws / jnp.sum(ws, axis=0, keepdims=True) * ROUTED_SCALE


def ffn_in(x, g, mods, router_wt, router_bias, tm=256):
    t, d = x.shape
    n_exp = router_wt.shape[0]
    row = lambda i: i
    return pl.pallas_call(
        functools.partial(_ffn_in_kernel, mods.tp // tm, n_exp),
        grid=(t // tm,),
        in_specs=[pl.BlockSpec((tm, d), lambda i: (i, 0)), pl.BlockSpec((1, d), lambda i: (0, 0)),
                  *mods.specs(3, tm, row), *mods.specs(4, tm, row),
                  pl.BlockSpec((n_exp, d), lambda i: (0, 0)),
                  pl.BlockSpec((n_exp, 1), lambda i: (0, 0))],
        out_specs=[pl.BlockSpec((tm, d // 2), lambda i: (i, 0)),
                   pl.BlockSpec((TOP_K, tm), lambda i: (0, i)),
                   pl.BlockSpec((TOP_K, tm), lambda i: (0, i))],
        out_shape=[jax.ShapeDtypeStruct((t, d // 2), U32),
                   jax.ShapeDtypeStruct((TOP_K, t), I32),
                   jax.ShapeDtypeStruct((TOP_K, t), F32)],
        compiler_params=_cp("arbitrary"),
        name="ffn_in",
    )(x, g.reshape(1, d), *mods.args(), *mods.args(), router_wt, router_bias.reshape(n_exp, 1))


def _dispatch_kernel(tb, tm, n_exp, n_blocks, slot_ref, tail_ref, used_ref, src_ref, o_hbm,
                     zeros, sem, zsem):
    i = pl.program_id(0)

    def zero_block(b):
        return pltpu.make_async_copy(zeros, o_hbm.at[pl.ds(pl.multiple_of(b * tm, tm), tm)], zsem)

    @pl.when(i == 0)
    def _():
        zeros[...] = jnp.zeros_like(zeros)
        used = used_ref[0]

        def start_tail(e, c):
            zero_block(tail_ref[e]).start()
            return c

        def start_unused(b, c):
            zero_block(b).start()
            return c

        def wait_one(b, c):
            zero_block(0).wait()
            return c

        lax.fori_loop(0, n_exp, start_tail, 0)
        lax.fori_loop(used, n_blocks, start_unused, 0)
        lax.fori_loop(0, n_exp + n_blocks - used, wait_one, 0)

    def issue(r, c):
        tok = i * tb + r
        for k in range(TOP_K):
            pltpu.make_async_copy(src_ref.at[pl.ds(r, 1)],
                                  o_hbm.at[pl.ds(slot_ref[tok * TOP_K + k], 1)], sem).start()
        return c

    lax.fori_loop(0, tb, issue, 0)
    for _ in range(TOP_K):
        pltpu.make_async_copy(src_ref, o_hbm.at[pl.ds(0, tb)], sem).wait()


def dispatch_rows(src, slots_flat, tail_blk, used_blocks, n_slots, tm, tb=128):
    t, w = src.shape
    n_exp = tail_blk.shape[0]
    n_blocks = n_slots // tm
    grid_spec = pltpu.PrefetchScalarGridSpec(
        num_scalar_prefetch=3,
        grid=(t // tb,),
        in_specs=[pl.BlockSpec((tb, w), lambda i, *_: (i, 0))],
        out_specs=pl.BlockSpec(memory_space=pl.ANY),
        scratch_shapes=[pltpu.VMEM((tm, w), src.dtype), pltpu.SemaphoreType.DMA(()),
                        pltpu.SemaphoreType.DMA(())])
    return pl.pallas_call(
        functools.partial(_dispatch_kernel, tb, tm, n_exp, n_blocks),
        grid_spec=grid_spec,
        out_shape=jax.ShapeDtypeStruct((n_slots, w), src.dtype),
        compiler_params=_cp("arbitrary"),
        name="dispatch_rows",
    )(slots_flat, tail_blk, used_blocks, src)


def _stream_group_weights(s, sf_ref, sg_ref, ng_ref, copies, consume):
    @pl.when(sf_ref[s] == 1)
    def _():
        g = sg_ref[s]

        @pl.when(g == 0)
        def _():
            for c in copies(0):
                c.start()

        for c in copies(g):
            c.wait()
        consume()

        @pl.when(g + 1 < ng_ref[0])
        def _():
            for c in copies(g + 1):
                c.start()


def _gate_up_kernel(layer, half, tn, sb_ref, so_ref, sf_ref, sg_ref, nv_ref, ge_ref, gj_ref, ng_ref,
                    x_ref, wg_hbm, wu_hbm, o_ref, stage, wg_s, wu_s, sem):
    s = pl.program_id(0)

    @pl.when(s >= nv_ref[0])
    def _():
        o_ref[...] = jnp.zeros_like(o_ref)

    def copies(g):
        col = pl.ds(pl.multiple_of(gj_ref[g] * tn, tn), tn)
        return (pltpu.make_async_copy(wg_hbm.at[layer, ge_ref[g], :, col], stage.at[0], sem.at[0]),
                pltpu.make_async_copy(wu_hbm.at[layer, ge_ref[g], :, col], stage.at[1], sem.at[0]))

    def consume():
        wg_s[...] = stage[0].astype(BF16)
        wu_s[...] = stage[1].astype(BF16)

    _stream_group_weights(s, sf_ref, sg_ref, ng_ref, copies, consume)

    @pl.when(s < nv_ref[0])
    def _():
        p = x_ref[...]
        lo = _unpack_lo(p).astype(BF16)
        hi = _unpack_hi(p).astype(BF16)
        gt = _dot(lo, wg_s[:half]) + _dot(hi, wg_s[half:])
        up = _dot(lo, wu_s[:half]) + _dot(hi, wu_s[half:])
        o_ref[...] = (_silu(gt) * up).astype(BF16)


def gate_up(x_packed, w_gate, w_up, layer, steps, tm, tn):
    n_slots, half = x_packed.shape
    _, _, d, f = w_gate.shape
    n_steps = steps[0].shape[0]
    grid_spec = pltpu.PrefetchScalarGridSpec(
        num_scalar_prefetch=len(steps),
        grid=(n_steps,),
        in_specs=[pl.BlockSpec((tm, half), lambda s, sb, *_: (sb[s], 0)),
                  pl.BlockSpec(memory_space=pl.ANY), pl.BlockSpec(memory_space=pl.ANY)],
        out_specs=pl.BlockSpec((tm, tn), lambda s, sb, so, *_: (sb[s], so[s])),
        scratch_shapes=[pltpu.VMEM((2, d, tn), F32), pltpu.VMEM((d, tn), BF16),
                        pltpu.VMEM((d, tn), BF16), pltpu.SemaphoreType.DMA((1,))])
    return pl.pallas_call(
        functools.partial(_gate_up_kernel, layer, half, tn),
        grid_spec=grid_spec,
        out_shape=jax.ShapeDtypeStruct((n_slots, f), BF16),
        compiler_params=_cp("arbitrary"),
        name="gate_up",
    )(*steps, x_packed, w_gate, w_up)


def _down_kernel(layer, tn, sb_ref, so_ref, sf_ref, sg_ref, nv_ref, ge_ref, gj_ref, ng_ref,
                 h_ref, w_hbm, o_ref, stage, w_s, sem):
    s = pl.program_id(0)

    @pl.when(s >= nv_ref[0])
    def _():
        o_ref[...] = jnp.zeros_like(o_ref)

    def copies(g):
        col = pl.ds(pl.multiple_of(gj_ref[g] * tn, tn), tn)
        return (pltpu.make_async_copy(w_hbm.at[layer, ge_ref[g], :, col], stage, sem.at[0]),)

    def consume():
        w_s[...] = stage[...].astype(BF16)

    _stream_group_weights(s, sf_ref, sg_ref, ng_ref, copies, consume)

    @pl.when(s < nv_ref[0])
    def _():
        y = _dot(h_ref[...], w_s[...])
        hw = y.shape[1] // 2
        o_ref[...] = _pack_words(y[:, :hw], y[:, hw:])


def down(h1, w_down, layer, steps, tm, tn):
    n_slots, f = h1.shape
    d = w_down.shape[3]
    n_steps = steps[0].shape[0]
    grid_spec = pltpu.PrefetchScalarGridSpec(
        num_scalar_prefetch=len(steps),
        grid=(n_steps,),
        in_specs=[pl.BlockSpec((tm, f), lambda s, sb, *_: (sb[s], 0)),
                  pl.BlockSpec(memory_space=pl.ANY)],
        out_specs=pl.BlockSpec((tm, tn // 2), lambda s, sb, so, *_: (sb[s], so[s])),
        scratch_shapes=[pltpu.VMEM((f, tn), F32), pltpu.VMEM((f, tn), BF16),
                        pltpu.SemaphoreType.DMA((1,))])
    return pl.pallas_call(
        functools.partial(_down_kernel, layer, tn),
        grid_spec=grid_spec,
        out_shape=jax.ShapeDtypeStruct((n_slots, d // 2), U32),
        compiler_params=_cp("arbitrary"),
        name="down",
    )(*steps, h1, w_down)


def _combine_kernel(npb, tb, tn, slot_ref, y_hbm, w_ref, sh_ref, x_ref, gp_ref, gs_ref, o_ref,
                    buf, sem):
    i = pl.program_id(0)
    n = pl.num_programs(0)

    def start_block(blk, slot):
        def body(r, c):
            tok = blk * tb + r
            for k in range(TOP_K):
                pltpu.make_async_copy(y_hbm.at[pl.ds(slot_ref[tok * TOP_K + k], 1)],
                                      buf.at[slot, pl.ds(k * tb + r, 1)], sem.at[slot]).start()
            return c
        lax.fori_loop(0, tb, body, 0)

    @pl.when(i == 0)
    def _():
        start_block(0, 0)

    @pl.when(i + 1 < n)
    def _():
        start_block(i + 1, (i + 1) % 2)

    cur = i % 2
    pltpu.make_async_copy(y_hbm.at[pl.ds(0, TOP_K * tb)], buf.at[cur], sem.at[cur]).wait()

    smp = i >= npb
    gate = lambda c0, c1: jnp.where(smp, gs_ref[:, c0:c1], gp_ref[0, :, c0:c1])
    w = w_ref[...]
    sh = sh_ref
    hw = tn // 2
    cw = min(hw, COMBINE_CHUNK)
    for c0 in range(0, sh_ref.shape[1], cw):
        a = (c0 // hw) * tn + c0 % hw
        lo = _unpack_lo(sh[:, c0:c0 + cw])
        hi = _unpack_hi(sh[:, c0:c0 + cw])
        for k in range(TOP_K):
            p = buf[cur, k * tb:(k + 1) * tb, c0:c0 + cw]
            wk = w[:, k:k + 1]
            lo = lo + wk * _unpack_lo(p)
            hi = hi + wk * _unpack_hi(p)
        o_ref[:, a:a + cw] = x_ref[:, a:a + cw] + gate(a, a + cw) * lo
        o_ref[:, a + hw:a + hw + cw] = (x_ref[:, a + hw:a + hw + cw]
                                        + gate(a + hw, a + hw + cw) * hi)


def combine(y_sorted, slots_flat, wts, y_shared, x, mods, k_gate, tn, tb=64):
    t, d = x.shape
    half = d // 2
    row = lambda i, sl: i
    grid_spec = pltpu.PrefetchScalarGridSpec(
        num_scalar_prefetch=1,
        grid=(t // tb,),
        in_specs=[pl.BlockSpec(memory_space=pl.ANY),
                  pl.BlockSpec((tb, TOP_K), lambda i, sl: (i, 0)),
                  pl.BlockSpec((tb, half), lambda i, sl: (i, 0)),
                  pl.BlockSpec((tb, d), lambda i, sl: (i, 0)),
                  *mods.specs(k_gate, tb, row)],
        out_specs=pl.BlockSpec((tb, d), lambda i, sl: (i, 0)),
        scratch_shapes=[pltpu.VMEM((2, TOP_K * tb, half), U32), pltpu.SemaphoreType.DMA((2,))])
    return pl.pallas_call(
        functools.partial(_combine_kernel, mods.tp // tb, tb, tn),
        grid_spec=grid_spec,
        out_shape=jax.ShapeDtypeStruct((t, d), F32),
        compiler_params=_cp("arbitrary"),
        name="combine",
    )(slots_flat, y_sorted, wts, y_shared, x, *mods.args())


def _dispatch_tables(idx, n_exp, tm, n_tiles):
    t, k = idx.shape
    a = t * k
    flat_e = idx.reshape(a)
    onehot = flat_e[:, None] == jnp.arange(n_exp, dtype=I32)[None, :]
    csum = jnp.cumsum(onehot.astype(I32), axis=0)
    counts = csum[-1]
    nblk = (counts + tm - 1) // tm
    blk_start = jnp.cumsum(nblk) - nblk
    slot = jnp.sum(jnp.where(onehot, (blk_start * tm)[None, :] + csum - 1, 0), axis=1)
    n_blocks = a // tm + n_exp
    n_slots = n_blocks * tm
    tail_blk = (blk_start + jnp.maximum(nblk - 1, 0)).astype(I32)
    used = jnp.sum(nblk).reshape(1).astype(I32)
    steps = [_grouped_steps(nblk, blk_start, n_blocks, nt) for nt in n_tiles]
    return slot.astype(I32), tail_blk, used, steps, n_slots


def _lookup(table, idx):
    n = table.shape[0]
    hit = idx[:, None] == jnp.arange(n, dtype=I32)[None, :]
    return jnp.sum(jnp.where(hit, table[None, :], 0), axis=1)


def _grouped_steps(nblk, blk_start, n_blocks, n_tiles):
    n_exp = nblk.shape[0]
    n_steps = n_blocks * n_tiles
    step_end = jnp.cumsum(nblk) * n_tiles
    total = step_end[-1]
    sidx = jnp.arange(n_steps, dtype=I32)
    valid = sidx < total
    s = jnp.minimum(sidx, total - 1)
    e = jnp.sum(step_end[None, :] <= s[:, None], axis=1).astype(I32)
    nb_e = _lookup(nblk, e)
    r = s - (_lookup(step_end, e) - nb_e * n_tiles)
    j = r // nb_e
    i = r % nb_e
    spare = sidx - total
    blk = jnp.where(valid, _lookup(blk_start, e) + i, jnp.sum(nblk) + spare // n_tiles)
    j_out = jnp.where(valid, j, spare % n_tiles)
    first = jnp.where((i == 0) & valid, 1, 0)
    used = (nblk > 0).astype(I32)
    used_before = jnp.cumsum(used) - used
    grp = _lookup(used_before, e) * n_tiles + j
    n_grp = jnp.sum(used) * n_tiles
    gidx = jnp.arange(n_exp * n_tiles, dtype=I32)
    g_e = jnp.minimum(jnp.sum((used_before + used)[None, :] <= (gidx // n_tiles)[:, None], axis=1),
                      n_exp - 1)
    as_i32 = lambda a: a.astype(I32)
    return (as_i32(blk), as_i32(j_out), as_i32(first), as_i32(grp), as_i32(total.reshape(1)),
            as_i32(g_e), as_i32(gidx % n_tiles), as_i32(n_grp.reshape(1)))


def _dense_steps(n_blocks, n_tiles):
    s = jnp.arange(n_blocks * n_tiles, dtype=I32)
    j = s // n_blocks
    i = s % n_blocks
    g = jnp.arange(n_tiles, dtype=I32)
    return (i, j, (i == 0).astype(I32), j, jnp.full((1,), n_blocks * n_tiles, I32),
            jnp.zeros_like(g), g, jnp.full((1,), n_tiles, I32))


def moe_ffn(x, g, mods, layer, router_wt, router_bias, w_gate, w_up, w_down, sw_gate, sw_up,
            sw_down, tl):
    t, d = x.shape
    _, n_exp, _, f = w_gate.shape
    tm, tn_up, tn_down = tl["moe_tm"], tl["tn_up"], tl["tn_down"]
    hp, idx_t, wts_t = ffn_in(x, g, mods, router_wt, router_bias, tm=tl["ffn_tm"])
    idx = idx_t.T
    wts = wts_t.T
    n_up, n_dn = f // tn_up, d // tn_down
    slot, tail_blk, used, (steps_up, steps_dn), n_slots = _dispatch_tables(idx, n_exp, tm,
                                                                           (n_up, n_dn))
    xs = dispatch_rows(hp, slot, tail_blk, used, n_slots, tm, tb=tl["dispatch_tb"])
    h1 = gate_up(xs, w_gate, w_up, layer, steps_up, tm, tn_up)
    ys = down(h1, w_down, layer, steps_dn, tm, tn_down)
    sh_steps_up = _dense_steps(t // tm, sw_gate.shape[3] // tn_up)
    sh_steps_dn = _dense_steps(t // tm, d // tn_down)
    s1 = gate_up(hp, sw_gate, sw_up, layer, sh_steps_up, tm, tn_up)
    ysh = down(s1, sw_down, layer, sh_steps_dn, tm, tn_down)
    return combine(ys, slot, wts, ysh, x, mods, 5, tn_down, tb=tl["comb_tb"])


def _rope_tables(pos):
    half = QK_ROPE // 2
    inv_freq = ROPE_THETA ** (-jnp.arange(half, dtype=F32) / half)
    ang = pos.astype(F32)[:, None] * inv_freq[None, :]
    z = jnp.zeros((pos.shape[0], LANES - QK_ROPE), F32)
    cos = jnp.concatenate([jnp.cos(ang), jnp.cos(ang), z], axis=1)
    sin = jnp.concatenate([jnp.sin(ang), jnp.sin(ang), z], axis=1)
    return cos, sin


def _rot_cols(w):
    half = w.shape[-1] // 2
    return jnp.concatenate([-w[..., half:], w[..., :half]], axis=-1)


def _pad_cols(w, n):
    return jnp.concatenate([w, jnp.zeros(w.shape[:-1] + (n - w.shape[-1],), w.dtype)], axis=-1)


TILES = dict(adaln_tn=512, norm_tm=256, mla_tm=256, qup_tm=512, kv_tm=512, kv_tn=1024,
             proj_tm=256, proj_tn=512, tq=1024, tk=1024, n_pg=32, pool_ts=256, pool_seqs=8,
             ffn_tm=256, dispatch_tb=128, moe_tm=256, tn_up=512, tn_down=4096, comb_tb=64)


def kernel(x_prompt, x_sample, cache_kv_latent, cache_k_rope, state_pool, page_table, c_prompt,
           c_sample, ada_w, ada_b, norm_mix_g, norm_ffn_g, mla_w_in, mla_q_norm_g, mla_w_uq,
           mla_kv_norm_g, mla_w_uk, mla_w_uv, mla_w_o, pool_w, pool_scale, router_w, router_bias,
           exp_w_gate, exp_w_up, exp_w_down, shared_w_gate, shared_w_up, shared_w_down,
           final_norm_g):
    return _step(TILES, x_prompt, x_sample, cache_kv_latent, cache_k_rope, state_pool, page_table,
                 c_prompt, c_sample, ada_w, ada_b, norm_mix_g, norm_ffn_g, mla_w_in, mla_q_norm_g,
                 mla_w_uq, mla_kv_norm_g, mla_w_uk, mla_w_uv, mla_w_o, pool_w, pool_scale,
                 router_w, router_bias, exp_w_gate, exp_w_up, exp_w_down, shared_w_gate,
                 shared_w_up, shared_w_down, final_norm_g)


def _step(tl, x_prompt, x_sample, cache_kv_latent, cache_k_rope, state_pool, page_table, c_prompt,
          c_sample, ada_w, ada_b, norm_mix_g, norm_ffn_g, mla_w_in, mla_q_norm_g, mla_w_uq,
          mla_kv_norm_g, mla_w_uk, mla_w_uv, mla_w_o, pool_w, pool_scale, router_w, router_bias,
          exp_w_gate, exp_w_up, exp_w_down, shared_w_gate, shared_w_up, shared_w_down,
          final_norm_g):
    n_p, seq, d = x_prompt.shape
    n_s, dec_seq, _ = x_sample.shape
    depth = ada_w.shape[0]
    tp, ts_tok = n_p * seq, n_s * dec_seq
    t = tp + ts_tok
    page = cache_kv_latent.shape[2]
    past = page_table.shape[1] * page
    q_lora = mla_q_norm_g.shape[1]
    kv_lora = mla_kv_norm_g.shape[1]
    n_h = N_HEADS
    sm_scale = (QK_NOPE + QK_ROPE) ** -0.5
    n_pool_hist = max(POOL_WINDOWS) - 1

    x = jnp.concatenate([x_prompt.reshape(tp, d), x_sample.reshape(ts_tok, d)], axis=0)

    n_c = n_p + n_s
    n_c_pad = -(-n_c // 8) * 8
    c_all = jnp.concatenate([c_prompt, c_sample, jnp.zeros((n_c_pad - n_c, d), F32)], axis=0)
    mod_all = adaln(c_all, ada_w, ada_b, tl["adaln_tn"])

    pos = jnp.concatenate([jnp.tile(jnp.arange(seq), n_p),
                           jnp.tile(past + jnp.arange(dec_seq), n_s)])
    cos_t, sin_t = _rope_tables(pos)

    lat_p, rope_p, pool_p, lat_s, rope_s, pool_s = [], [], [], [], [], []
    for i in range(depth):
        j = i // 2
        mods = Mods(mod_all[i, :n_p].reshape(n_p, 1, 6 * d),
                    jnp.repeat(mod_all[i, n_p:n_c], dec_seq, axis=0), seq, tp, d)
        if i % 2 == 0:
            w_in = mla_w_in[j]
            w_r = w_in[:, q_lora + kv_lora:]
            w_aug = jnp.concatenate([w_in[:, :q_lora + kv_lora], _pad_cols(w_r, LANES),
                                     _pad_cols(_rot_cols(w_r), LANES)], axis=1).astype(BF16)
            cq, ckv, ckv_b, kr, kr_b = mla_in(x, norm_mix_g[i], mods, w_aug, mla_q_norm_g[j],
                                              mla_kv_norm_g[j], cos_t, sin_t, q_lora, kv_lora,
                                              tm=tl["mla_tm"])
            wq = mla_w_uq[j].reshape(q_lora, n_h, QK_NOPE + QK_ROPE)
            wq_r = wq[:, :, QK_NOPE:]
            w1 = jnp.concatenate([wq, jnp.zeros((q_lora, n_h, HEAD_PAD - QK_NOPE - QK_ROPE), F32)],
                                 axis=2).reshape(q_lora, n_h * HEAD_PAD).astype(BF16)
            w2 = _pad_cols(_rot_cols(wq_r), LANES).reshape(q_lora, n_h * LANES).astype(BF16)
            q_cat = q_up(cq, w1, w2, cos_t, sin_t, tm=tl["qup_tm"])

            w_kv = jnp.concatenate([mla_w_uk[j], mla_w_uv[j]], axis=1).astype(BF16)
            kv = matmul(ckv_b[:tp], w_kv, BF16, tl["kv_tm"], tl["kv_tn"])
            attn_p = prompt_attention(q_cat, kv, kr_b, n_p, seq, sm_scale, tl["tq"], tl["tk"])

            w_uk_t = mla_w_uk[j].reshape(kv_lora, n_h, QK_NOPE).transpose(1, 2, 0).astype(BF16)
            w_uv_h = mla_w_uv[j].reshape(kv_lora, n_h, V_HEAD).transpose(1, 0, 2).astype(BF16)
            q_s = q_cat[tp:].reshape(ts_tok, n_h, HEAD_PAD)
            q_nope_s = q_s[:, :, :QK_NOPE].transpose(1, 0, 2)
            q_lat = heads_matmul(
                q_nope_s, w_uk_t,
                pl.BlockSpec((1, ts_tok, QK_NOPE), lambda h: (h, 0, 0)),
                pl.BlockSpec((1, ts_tok, kv_lora), lambda h: (h, 0, 0)),
                jax.ShapeDtypeStruct((n_h, ts_tok, kv_lora), BF16))
            q_lat = q_lat.transpose(1, 0, 2).reshape(n_s, dec_seq * n_h, kv_lora)
            q_rope_s = q_s[:, :, QK_NOPE:QK_NOPE + QK_ROPE].reshape(n_s, dec_seq * n_h, QK_ROPE)
            kpad = page - dec_seq
            new_lat = jnp.pad(ckv_b[tp:].reshape(n_s, dec_seq, kv_lora), ((0, 0), (0, kpad), (0, 0)))
            new_rope = jnp.pad(kr_b[tp:, :QK_ROPE].reshape(n_s, dec_seq, QK_ROPE),
                               ((0, 0), (0, kpad), (0, 0))).swapaxes(1, 2)
            o_lat = sample_attention(q_lat, q_rope_s, cache_kv_latent,
                                     cache_k_rope.swapaxes(2, 3), j,
                                     page_table, new_lat, new_rope, dec_seq, sm_scale,
                                     n_pg=tl["n_pg"])
            o_lat = o_lat.reshape(ts_tok, n_h, kv_lora).transpose(1, 0, 2)
            attn_s = heads_matmul(
                o_lat, w_uv_h,
                pl.BlockSpec((1, ts_tok, kv_lora), lambda h: (h, 0, 0)),
                pl.BlockSpec((ts_tok, V_HEAD), lambda h: (0, h)),
                jax.ShapeDtypeStruct((ts_tok, n_h * V_HEAD), BF16))
            attn = jnp.concatenate([attn_p, attn_s], axis=0)
            x = proj_residual(attn, mla_w_o[j].astype(BF16), x, mods, 2, tl["proj_tm"],
                              tl["proj_tn"])

            lat_p.append(ckv[:tp].reshape(n_p, seq, kv_lora))
            rope_p.append(kr[:tp, :QK_ROPE].reshape(n_p, seq, QK_ROPE))
            lat_s.append(ckv[tp:].reshape(n_s, dec_seq, kv_lora))
            rope_s.append(kr[tp:, :QK_ROPE].reshape(n_s, dec_seq, QK_ROPE))
        else:
            h = modnorm(x, norm_mix_g[i], mods, 0, 1, F32, tm=tl["norm_tm"])
            pw = pool_w[j].astype(BF16)
            gate_p = mods.mp[:, :, 2 * d:3 * d]
            xs3 = x[tp:].reshape(n_s, dec_seq, d)
            x_new = pool_mixer(h, x, gate_p, pw, pool_scale[j], n_p, seq, 0, tl["pool_ts"],
                               in_place=True)
            hs3 = h[tp:].reshape(n_s, dec_seq, d)
            rows = POOL_HALO + 8
            tail = rows - POOL_HALO - dec_seq
            lead = POOL_HALO - n_pool_hist
            hh = jnp.concatenate([jnp.zeros((n_s, lead, d), F32), state_pool[j], hs3,
                                  jnp.zeros((n_s, tail, d), F32)], axis=1)
            pad3 = lambda a: jnp.pad(a, ((0, 0), (POOL_HALO, tail), (0, 0)))
            gate_s = mods.ms[:, 2 * d:3 * d].reshape(n_s, dec_seq, d)
            seqs = tl["pool_seqs"]
            xs_new = pool_mixer(hh.reshape(n_s * rows, d), pad3(xs3).reshape(n_s * rows, d),
                                pad3(gate_s).reshape(n_s * rows, d), pw, pool_scale[j],
                                n_s // seqs, seqs * rows, past - POOL_HALO, seqs * rows, period=rows)
            xs_new = xs_new.reshape(n_s, rows, d)[:, POOL_HALO:POOL_HALO + dec_seq]
            x = lax.dynamic_update_slice(x_new, xs_new.reshape(ts_tok, d), (tp, 0))
            pool_p.append(h[:tp].reshape(n_p, seq, d)[:, seq - n_pool_hist:])
            pool_s.append(jnp.concatenate([state_pool[j], hs3], axis=1)[:, dec_seq:])

        x = moe_ffn(x, norm_ffn_g[i], mods, i, router_w[i].T, router_bias[i], exp_w_gate,
                    exp_w_up, exp_w_down, shared_w_gate[:, None], shared_w_up[:, None],
                    shared_w_down[:, None], tl)

    y_p, y_s = rmsnorm_split(x, final_norm_g, tp, tm=tl["norm_tm"])
    return (y_p.reshape(n_p, seq, d), y_s.reshape(n_s, dec_seq, d),
            jnp.stack(lat_p), jnp.stack(rope_p), jnp.stack(pool_p),
            jnp.stack(lat_s), jnp.stack(rope_s), jnp.stack(pool_s))
```

```python
import functools

import jax
import jax.numpy as jnp
from jax import lax
from jax.experimental import pallas as pl
from jax.experimental.pallas import tpu as pltpu

F32 = jnp.float32
BF16 = jnp.bfloat16
U32 = jnp.uint32
I32 = jnp.int32

EPS = 1e-6
ROPE_THETA = 10000.0
ROUTED_SCALE = 2.5
N_GROUPS = 8
TOPK_GROUPS = 4
TOP_K = 8
POOL_WINDOWS = (2, 4, 8, 16)
POOL_HALO = 16
N_HEADS = 32
QK_NOPE = 128
QK_ROPE = 64
V_HEAD = 128
LANES = 128
HEAD_PAD = 2 * LANES
VMEM_LIMIT = 56 * 1024 * 1024
FLASH_ROWS = 256
COMBINE_CHUNK = 256
NEG_BIG = -1e30
LOG2_E = 1.4426950408889634


def _cp(*sem, vmem=VMEM_LIMIT):
    return pltpu.CompilerParams(dimension_semantics=sem, vmem_limit_bytes=vmem)


def _dot(a, b):
    return jnp.dot(a, b, preferred_element_type=F32)


def _dot_nt(a, b):
    return lax.dot_general(a, b, (((1,), (1,)), ((), ())), preferred_element_type=F32)


def _rms(x, g):
    return x * lax.rsqrt(jnp.mean(x * x, axis=-1, keepdims=True) + EPS) * g


def _silu(x):
    return x * jax.nn.sigmoid(x)


def _unpack_lo(p):
    return lax.bitcast_convert_type(p << 16, F32)


def _unpack_hi(p):
    return lax.bitcast_convert_type(p & jnp.uint32(0xFFFF0000), F32)


def _pack_words(lo, hi):
    lo_b = lax.bitcast_convert_type(lo.astype(BF16).astype(F32), U32)
    hi_b = lax.bitcast_convert_type(hi.astype(BF16).astype(F32), U32)
    return (hi_b & jnp.uint32(0xFFFF0000)) | (lo_b >> 16)


def _adaln_kernel(c_ref, w_ref, b_ref, o_ref, s_ref):
    @pl.when(jnp.logical_and(pl.program_id(0) == 0, pl.program_id(1) == 0))
    def _():
        s_ref[...] = _silu(c_ref[...]).astype(BF16)

    o_ref[0] = _dot(s_ref[...], w_ref[0].astype(BF16)) + b_ref[0]


def adaln(c_all, ada_w, ada_b, tn=512):
    depth, d, n = ada_w.shape
    r = c_all.shape[0]
    return pl.pallas_call(
        _adaln_kernel,
        grid=(depth, n // tn),
        in_specs=[pl.BlockSpec((r, d), lambda l, j: (0, 0)),
                  pl.BlockSpec((1, d, tn), lambda l, j: (l, 0, j)),
                  pl.BlockSpec((1, 1, tn), lambda l, j: (l, 0, j))],
        out_specs=pl.BlockSpec((1, r, tn), lambda l, j: (l, 0, j)),
        out_shape=jax.ShapeDtypeStruct((depth, r, n), F32),
        scratch_shapes=[pltpu.VMEM((r, d), BF16)],
        compiler_params=_cp("arbitrary", "arbitrary"),
        name="adaln",
    )(c_all, ada_w, ada_b.reshape(depth, 1, n))


class Mods:
    def __init__(self, mp, ms, seq, n_prompt_tok, d):
        self.mp, self.ms, self.seq, self.tp, self.d = mp, ms, seq, n_prompt_tok, d

    def specs(self, k, tm, row_of):
        npb = self.tp // tm
        per_seq = self.seq // tm
        n_p = self.mp.shape[0]

        def p_map(*g):
            return (jnp.minimum(row_of(*g) // per_seq, n_p - 1), 0, k)

        def s_map(*g):
            return (jnp.maximum(row_of(*g) - npb, 0), k)

        return [pl.BlockSpec((1, 1, self.d), p_map), pl.BlockSpec((tm, self.d), s_map)]

    def args(self):
        return [self.mp, self.ms]


def _pick(is_sample, p_ref, s_ref):
    return jnp.where(is_sample, s_ref[...], p_ref[0])


def _modnorm_kernel(npb, x_ref, g_ref, shp_ref, shs_ref, scp_ref, scs_ref, o_ref):
    smp = pl.program_id(0) >= npb
    shift = _pick(smp, shp_ref, shs_ref)
    scale = _pick(smp, scp_ref, scs_ref)
    o_ref[...] = (_rms(x_ref[...], g_ref[...]) * (1.0 + scale) + shift).astype(o_ref.dtype)


def modnorm(x, g, mods, k_shift, k_scale, out_dtype, tm=256):
    t, d = x.shape
    row = lambda i: i
    return pl.pallas_call(
        functools.partial(_modnorm_kernel, mods.tp // tm),
        grid=(t // tm,),
        in_specs=[pl.BlockSpec((tm, d), lambda i: (i, 0)),
                  pl.BlockSpec((1, d), lambda i: (0, 0)),
                  *mods.specs(k_shift, tm, row), *mods.specs(k_scale, tm, row)],
        out_specs=pl.BlockSpec((tm, d), lambda i: (i, 0)),
        out_shape=jax.ShapeDtypeStruct((t, d), out_dtype),
        compiler_params=_cp("arbitrary"),
        name="modnorm",
    )(x, g.reshape(1, d), *mods.args(), *mods.args())


def _rmsnorm_kernel(npb, x_ref, g_ref, op_ref, os_ref):
    i = pl.program_id(0)
    y = _rms(x_ref[...], g_ref[...])

    @pl.when(i < npb)
    def _():
        op_ref[...] = y

    @pl.when(i >= npb)
    def _():
        os_ref[...] = y


def rmsnorm_split(x, g, tp, tm=256):
    t, d = x.shape
    npb = tp // tm
    return pl.pallas_call(
        functools.partial(_rmsnorm_kernel, npb),
        grid=(t // tm,),
        in_specs=[pl.BlockSpec((tm, d), lambda i: (i, 0)), pl.BlockSpec((1, d), lambda i: (0, 0))],
        out_specs=[pl.BlockSpec((tm, d), lambda i: (jnp.minimum(i, npb - 1), 0)),
                   pl.BlockSpec((tm, d), lambda i: (jnp.maximum(i - npb, 0), 0))],
        out_shape=[jax.ShapeDtypeStruct((tp, d), F32), jax.ShapeDtypeStruct((t - tp, d), F32)],
        compiler_params=_cp("arbitrary"),
        name="final_rmsnorm",
    )(x, g.reshape(1, d))


def _split_row_specs(parts, tm, width, npb, row_of, col_of=lambda *g: 0):
    def p_map(*g):
        return (jnp.minimum(row_of(*g), npb - 1), col_of(*g))

    def s_map(*g):
        return (jnp.maximum(row_of(*g) - npb, 0), col_of(*g))

    return [pl.BlockSpec((tm, width), p_map), pl.BlockSpec((tm, width), s_map)]


def _pick_rows(is_sample, p_ref, s_ref):
    return jnp.where(is_sample, s_ref[...], p_ref[...])


def _mla_in_kernel(npb, q_lora, kv_lora, xp_ref, xs_ref, g_ref, shp_ref, shs_ref, scp_ref, scs_ref,
                   w_ref, gq_ref, gkv_ref, cos_ref, sin_ref,
                   cq_ref, ckv_ref, ckvb_ref, kr_ref, krb_ref):
    smp = pl.program_id(0) >= npb
    shift = _pick(smp, shp_ref, shs_ref)
    scale = _pick(smp, scp_ref, scs_ref)
    x = _pick_rows(smp, xp_ref, xs_ref)
    h = (_rms(x, g_ref[...]) * (1.0 + scale) + shift).astype(BF16)
    a = _dot(h, w_ref[...])
    cq_ref[...] = _rms(a[:, :q_lora], gq_ref[...]).astype(BF16)
    ckv = _rms(a[:, q_lora:q_lora + kv_lora], gkv_ref[...])
    ckv_ref[...] = ckv
    ckvb_ref[...] = ckv.astype(BF16)
    o = q_lora + kv_lora
    kr = a[:, o:o + LANES] * cos_ref[...] + a[:, o + LANES:o + 2 * LANES] * sin_ref[...]
    kr_ref[...] = kr
    krb_ref[...] = kr.astype(BF16)


def mla_in(x_parts, g, mods, w_aug, gq, gkv, cos_t, sin_t, q_lora, kv_lora, tm=256):
    d = x_parts[0].shape[1]
    t = x_parts[0].shape[0] + x_parts[1].shape[0]
    n_aug = w_aug.shape[1]
    row = lambda i: i
    blk = lambda w: pl.BlockSpec((tm, w), lambda i: (i, 0))
    return pl.pallas_call(
        functools.partial(_mla_in_kernel, mods.tp // tm, q_lora, kv_lora),
        grid=(t // tm,),
        in_specs=[*_split_row_specs(x_parts, tm, d, mods.tp // tm, row),
                  pl.BlockSpec((1, d), lambda i: (0, 0)),
                  *mods.specs(0, tm, row), *mods.specs(1, tm, row),
                  pl.BlockSpec((d, n_aug), lambda i: (0, 0)),
                  pl.BlockSpec((1, q_lora), lambda i: (0, 0)),
                  pl.BlockSpec((1, kv_lora), lambda i: (0, 0)),
                  blk(LANES), blk(LANES)],
        out_specs=[blk(q_lora), blk(kv_lora), blk(kv_lora), blk(LANES), blk(LANES)],
        out_shape=[jax.ShapeDtypeStruct((t, q_lora), BF16),
                   jax.ShapeDtypeStruct((t, kv_lora), F32),
                   jax.ShapeDtypeStruct((t, kv_lora), BF16),
                   jax.ShapeDtypeStruct((t, LANES), F32),
                   jax.ShapeDtypeStruct((t, LANES), BF16)],
        compiler_params=_cp("arbitrary"),
        name="mla_in",
    )(*x_parts, g.reshape(1, d), *mods.args(), *mods.args(), w_aug, gq.reshape(1, -1),
      gkv.reshape(1, -1), cos_t, sin_t)


def _q_up_kernel(hg, cq_ref, w1_ref, w2_ref, cos_ref, sin_ref, o_ref):
    cq = cq_ref[...]
    a = _dot(cq, w1_ref[...])
    r = _dot(cq, w2_ref[...])
    c = cos_ref[...]
    s = sin_ref[...]
    for h in range(hg):
        b = h * HEAD_PAD
        o_ref[:, b:b + LANES] = a[:, b:b + LANES].astype(BF16)
        o_ref[:, b + LANES:b + HEAD_PAD] = (
            a[:, b + LANES:b + HEAD_PAD] * c + r[:, h * LANES:(h + 1) * LANES] * s).astype(BF16)


def q_up(cq, w1, w2, cos_t, sin_t, tm=512, hg=4):
    t, k = cq.shape
    n_h = w1.shape[1] // HEAD_PAD
    return pl.pallas_call(
        functools.partial(_q_up_kernel, hg),
        grid=(n_h // hg, t // tm),
        in_specs=[pl.BlockSpec((tm, k), lambda j, i: (i, 0)),
                  pl.BlockSpec((k, hg * HEAD_PAD), lambda j, i: (0, j)),
                  pl.BlockSpec((k, hg * LANES), lambda j, i: (0, j)),
                  pl.BlockSpec((tm, LANES), lambda j, i: (i, 0)),
                  pl.BlockSpec((tm, LANES), lambda j, i: (i, 0))],
        out_specs=pl.BlockSpec((tm, hg * HEAD_PAD), lambda j, i: (i, j)),
        out_shape=jax.ShapeDtypeStruct((t, n_h * HEAD_PAD), BF16),
        compiler_params=_cp("arbitrary", "arbitrary"),
        name="q_up",
    )(cq, w1, w2, cos_t, sin_t)


def _mm_kernel(a_ref, b_ref, o_ref):
    o_ref[...] = _dot(a_ref[...], b_ref[...]).astype(o_ref.dtype)


def matmul(a, b, out_dtype, tm=512, tn=1024):
    m, k = a.shape
    n = b.shape[1]
    tm, tn = min(tm, m), min(tn, n)
    return pl.pallas_call(
        _mm_kernel,
        grid=(n // tn, m // tm),
        in_specs=[pl.BlockSpec((tm, k), lambda j, i: (i, 0)),
                  pl.BlockSpec((k, tn), lambda j, i: (0, j))],
        out_specs=pl.BlockSpec((tm, tn), lambda j, i: (i, j)),
        out_shape=jax.ShapeDtypeStruct((m, n), out_dtype),
        compiler_params=_cp("arbitrary", "arbitrary"),
        name="matmul",
    )(a, b)


def _proj_res_kernel(npb, ap_ref, as_ref, w_ref, xp_ref, xs_ref, gp_ref, gs_ref, o_ref):
    smp = pl.program_id(1) >= npb
    gate = _pick(smp, gp_ref, gs_ref)
    a = _pick_rows(smp, ap_ref, as_ref)
    o_ref[...] = _pick_rows(smp, xp_ref, xs_ref) + gate * _dot(a, w_ref[...])


def proj_residual(a_parts, w, x_parts, mods, k_gate, tm=256, tn=512):
    k = a_parts[0].shape[1]
    t = a_parts[0].shape[0] + a_parts[1].shape[0]
    d = w.shape[1]
    npb = mods.tp // tm
    per_seq = mods.seq // tm
    n_p = mods.mp.shape[0]
    cb = mods.d // tn
    gp = pl.BlockSpec((1, 1, tn), lambda j, i: (jnp.minimum(i // per_seq, n_p - 1), 0, k_gate * cb + j))
    gs = pl.BlockSpec((tm, tn), lambda j, i: (jnp.maximum(i - npb, 0), k_gate * cb + j))
    return pl.pallas_call(
        functools.partial(_proj_res_kernel, npb),
        grid=(d // tn, t // tm),
        in_specs=[*_split_row_specs(a_parts, tm, k, npb, lambda j, i: i),
                  pl.BlockSpec((k, tn), lambda j, i: (0, j)),
                  *_split_row_specs(x_parts, tm, tn, npb, lambda j, i: i, lambda j, i: j), gp, gs],
        out_specs=pl.BlockSpec((tm, tn), lambda j, i: (i, j)),
        out_shape=jax.ShapeDtypeStruct((t, d), F32),
        compiler_params=_cp("arbitrary", "arbitrary"),
        name="proj_residual",
    )(*a_parts, w, *x_parts, *mods.args())


def _flash_kernel(tq, tk, scale, q_ref, kn_ref, kr_ref, v_ref, o_ref, m_s, l_s, acc_s):
    qi = pl.program_id(2)
    ki = pl.program_id(3)
    nk = pl.num_programs(3)

    @pl.when(ki == 0)
    def _():
        m_s[...] = jnp.full_like(m_s, NEG_BIG)
        l_s[...] = jnp.zeros_like(l_s)
        acc_s[...] = jnp.zeros_like(acc_s)

    def update(masked):
        k = jnp.concatenate([kn_ref[...], kr_ref[...]], axis=1)
        v = v_ref[...]
        for r0 in range(0, tq, FLASH_ROWS):
            rows = slice(r0, min(r0 + FLASH_ROWS, tq))
            nr = rows.stop - rows.start
            s = _dot_nt(q_ref[rows, :], k) * (scale * LOG2_E)
            if masked:
                qpos = qi * tq + r0 + lax.broadcasted_iota(I32, (nr, tk), 0)
                kpos = ki * tk + lax.broadcasted_iota(I32, (nr, tk), 1)
                s = jnp.where(kpos <= qpos, s, NEG_BIG)
            m_old = m_s[rows, :]
            m_new = jnp.maximum(m_old, jnp.max(s, axis=-1, keepdims=True))
            alpha = jnp.exp2(m_old - m_new)
            p = jnp.exp2(s - m_new)
            l_s[rows, :] = alpha * l_s[rows, :] + jnp.sum(p, axis=-1, keepdims=True)
            acc_s[rows, :] = alpha * acc_s[rows, :] + _dot(p.astype(BF16), v)
            m_s[rows, :] = m_new

    live = ki * tk <= qi * tq + (tq - 1)
    crosses = ki * tk + (tk - 1) > qi * tq

    @pl.when(jnp.logical_and(live, crosses))
    def _():
        update(True)

    @pl.when(jnp.logical_and(live, jnp.logical_not(crosses)))
    def _():
        update(False)

    @pl.when(ki == nk - 1)
    def _():
        o_ref[...] = (acc_s[...] / l_s[...]).astype(o_ref.dtype)


def prompt_attention(q_cat, kv, kr_b, n_seq, seq, scale, tq=1024, tk=1024):
    n_h = q_cat.shape[1] // HEAD_PAD
    tq, tk = min(tq, seq), min(tk, seq)
    nq, nk = seq // tq, seq // tk

    def kmap(b, h, qi, ki):
        return jnp.minimum(ki, (qi * tq + tq - 1) // tk)

    return pl.pallas_call(
        functools.partial(_flash_kernel, tq, tk, scale),
        grid=(n_seq, n_h, nq, nk),
        in_specs=[pl.BlockSpec((tq, HEAD_PAD), lambda b, h, qi, ki: (b * nq + qi, h)),
                  pl.BlockSpec((tk, QK_NOPE), lambda b, h, qi, ki: (b * nk + kmap(b, h, qi, ki), h)),
                  pl.BlockSpec((tk, LANES), lambda b, h, qi, ki: (b * nk + kmap(b, h, qi, ki), 0)),
                  pl.BlockSpec((tk, V_HEAD), lambda b, h, qi, ki: (b * nk + kmap(b, h, qi, ki), n_h + h))],
        out_specs=pl.BlockSpec((tq, V_HEAD), lambda b, h, qi, ki: (b * nq + qi, h)),
        out_shape=jax.ShapeDtypeStruct((n_seq * seq, n_h * V_HEAD), BF16),
        scratch_shapes=[pltpu.VMEM((tq, 1), F32), pltpu.VMEM((tq, 1), F32),
                        pltpu.VMEM((tq, V_HEAD), F32)],
        compiler_params=_cp("arbitrary", "arbitrary", "arbitrary", "arbitrary"),
        name="prompt_attention",
    )(q_cat, kv, kr_b, kv)


def _bmm_kernel(a_ref, b_ref, o_ref):
    a = a_ref[...]
    a = a.reshape(a.shape[-2:])
    o = _dot(a, b_ref[0]).astype(o_ref.dtype)
    o_ref[...] = o.reshape(o_ref.shape)


def heads_matmul(a, b, a_spec, out_spec, out_shape):
    n_h = b.shape[0]
    return pl.pallas_call(
        _bmm_kernel,
        grid=(n_h,),
        in_specs=[a_spec, pl.BlockSpec((1,) + b.shape[1:], lambda h: (h, 0, 0))],
        out_specs=out_spec,
        out_shape=out_shape,
        compiler_params=_cp("arbitrary"),
        name="heads_matmul",
    )(a, b)


def _decode_kernel(layer, n_pg, n_grp, dec_seq, scale, pt_ref, ql_ref, qr_ref, lat_hbm, rope_hbm,
                   nlat_ref, nrope_ref, o_ref, lat_buf, rope_buf, sem, m_s, l_s, acc_s):
    i = pl.program_id(0)
    g = pl.program_id(1)
    n_seq = pl.num_programs(0)
    page = lat_hbm.shape[2]

    def copies(seq, grp, slot):
        out = []
        for k in range(n_pg):
            pg = pt_ref[seq, grp * n_pg + k]
            out.append(pltpu.make_async_copy(lat_hbm.at[layer, pg],
                                             lat_buf.at[slot, pl.ds(k * page, page)], sem.at[slot]))
            out.append(pltpu.make_async_copy(rope_hbm.at[layer, pg],
                                             rope_buf.at[slot, :, pl.ds(k * page, page)],
                                             sem.at[slot]))
        return out

    step = i * n_grp + g
    slot = step % 2

    @pl.when(jnp.logical_and(i == 0, g == 0))
    def _():
        for c in copies(0, 0, 0):
            c.start()

    @pl.when(jnp.logical_and(g < n_grp, step + 1 < n_seq * n_grp))
    def _():
        nxt = step + 1
        for c in copies(nxt // n_grp, nxt % n_grp, nxt % 2):
            c.start()

    @pl.when(g == 0)
    def _():
        m_s[...] = jnp.full_like(m_s, NEG_BIG)
        l_s[...] = jnp.zeros_like(l_s)
        acc_s[...] = jnp.zeros_like(acc_s)

    def update(lat, rope_t, mask):
        s = (_dot_nt(ql_ref[0], lat) + _dot(qr_ref[0], rope_t)) * (scale * LOG2_E)
        if mask is not None:
            s = jnp.where(mask, s, NEG_BIG)
        m_old = m_s[...]
        m_new = jnp.maximum(m_old, jnp.max(s, axis=-1, keepdims=True))
        alpha = jnp.exp2(m_old - m_new)
        p = jnp.exp2(s - m_new)
        l_s[...] = alpha * l_s[...] + jnp.sum(p, axis=-1, keepdims=True)
        acc_s[...] = alpha * acc_s[...] + _dot(p.astype(BF16), lat)
        m_s[...] = m_new

    @pl.when(g < n_grp)
    def _():
        for c in copies(i, g, slot):
            c.wait()
        update(lat_buf[slot].astype(BF16), rope_buf[slot].astype(BF16), None)

    @pl.when(g == n_grp)
    def _():
        lat = nlat_ref[0]
        rows, keys = ql_ref.shape[1], lat.shape[0]
        q_tok = lax.broadcasted_iota(I32, (rows, keys), 0) // (rows // dec_seq)
        k_tok = lax.broadcasted_iota(I32, (rows, keys), 1)
        update(lat, nrope_ref[0], k_tok <= q_tok)
        o_ref[0] = (acc_s[...] / l_s[...]).astype(o_ref.dtype)


def sample_attention(q_lat, q_rope, cache_lat, cache_rope, layer, page_table, new_lat, new_rope,
                     dec_seq, scale, n_pg=16):
    n, r, c = q_lat.shape
    page = cache_lat.shape[2]
    rd = cache_rope.shape[2]
    n_pages = page_table.shape[1]
    n_pg = min(n_pg, n_pages)
    n_grp = n_pages // n_pg
    kn = new_lat.shape[1]

    seq_map = lambda i, g, pt: (i, 0, 0)
    in_specs = [pl.BlockSpec((1, r, c), seq_map), pl.BlockSpec((1, r, rd), seq_map),
                pl.BlockSpec(memory_space=pl.ANY), pl.BlockSpec(memory_space=pl.ANY),
                pl.BlockSpec((1, kn, c), seq_map), pl.BlockSpec((1, rd, kn), seq_map)]
    grid_spec = pltpu.PrefetchScalarGridSpec(
        num_scalar_prefetch=1,
        grid=(n, n_grp + 1),
        in_specs=in_specs,
        out_specs=pl.BlockSpec((1, r, c), seq_map),
        scratch_shapes=[pltpu.VMEM((2, n_pg * page, c), F32), pltpu.VMEM((2, rd, n_pg * page), F32),
                        pltpu.SemaphoreType.DMA((2,)),
                        pltpu.VMEM((r, 1), F32), pltpu.VMEM((r, 1), F32), pltpu.VMEM((r, c), F32)])
    return pl.pallas_call(
        functools.partial(_decode_kernel, layer, n_pg, n_grp, dec_seq, scale),
        grid_spec=grid_spec,
        out_shape=jax.ShapeDtypeStruct((n, r, c), BF16),
        compiler_params=_cp("arbitrary", "arbitrary"),
        name="sample_attention",
    )(page_table, q_lat, q_rope, cache_lat, cache_rope, new_lat, new_rope)


def _pool_kernel(ts, grp_w, first_pos, period, gate_per_seq, h_ref, halo_ref, x_ref, gate_ref, w_ref,
                 ps_ref, o_ref):
    i = pl.program_id(1)
    keep = jnp.where(i > 0, 1.0, 0.0).astype(F32)
    u_all = jnp.concatenate([halo_ref[...] * keep, h_ref[...]], axis=0)
    rows = POOL_HALO + ts
    r = i * ts - POOL_HALO + lax.broadcasted_iota(I32, (rows, 1), 0)
    if period:
        r = (r + period) % period
    pos = first_pos + r
    for g, win in enumerate(POOL_WINDOWS):
        u = u_all[:, g * grp_w:(g + 1) * grp_w]
        s = u
        k = 1
        while k < win:
            s = s + pltpu.roll(s, k, axis=0)
            k *= 2
        count = jnp.clip(pos + 1, 1, win).astype(F32)
        dlt = (s / count - u)[POOL_HALO:].astype(BF16)
        cols = slice(g * grp_w, (g + 1) * grp_w)
        y = _dot(dlt, w_ref[g]) * ps_ref[:, cols]
        gate = gate_ref[0, :, cols] if gate_per_seq else gate_ref[:, cols]
        o_ref[:, cols] = x_ref[:, cols] + gate * y


def pool_mixer(h, x, gate, pool_w, pool_scale, n_seq, seq_len, first_pos, ts, period=0,
               in_place=False):
    d = h.shape[1]
    grp_w = d // len(POOL_WINDOWS)
    nb = seq_len // ts
    hb = ts // POOL_HALO
    row_map = lambda b, i: (b * nb + i, 0)
    gate_per_seq = gate.ndim == 3
    gate_spec = (pl.BlockSpec((1, 1, d), lambda b, i: (b, 0, 0)) if gate_per_seq
                 else pl.BlockSpec((ts, d), row_map))
    return pl.pallas_call(
        functools.partial(_pool_kernel, ts, grp_w, first_pos, period, gate_per_seq),
        grid=(n_seq, nb),
        in_specs=[pl.BlockSpec((ts, d), row_map),
                  pl.BlockSpec((POOL_HALO, d), lambda b, i: (jnp.maximum((b * nb + i) * hb - 1, 0), 0)),
                  pl.BlockSpec((ts, d), row_map),
                  gate_spec,
                  pl.BlockSpec(pool_w.shape, lambda b, i: (0, 0, 0)),
                  pl.BlockSpec((1, d), lambda b, i: (0, 0))],
        out_specs=pl.BlockSpec((ts, d), row_map),
        out_shape=jax.ShapeDtypeStruct(x.shape, F32),
        input_output_aliases={2: 0} if in_place else {},
        compiler_params=_cp("arbitrary", "arbitrary"),
        name="pool_mixer",
    )(h, h, x, gate, pool_w, pool_scale.reshape(1, d))


def _ffn_in_kernel(npb, n_exp, x_ref, g_ref, shp_ref, shs_ref, scp_ref, scs_ref, rw_ref, rb_ref,
                   h_ref, idx_ref, wts_ref):
    smp = pl.program_id(0) >= npb
    shift = _pick(smp, shp_ref, shs_ref)
    scale = _pick(smp, scp_ref, scs_ref)
    h = _rms(x_ref[...], g_ref[...]) * (1.0 + scale) + shift
    half = h.shape[1] // 2
    h_ref[...] = _pack_words(h[:, :half], h[:, half:])
    logits = lax.dot_general(rw_ref[...], h, (((1,), (1,)), ((), ())),
                             preferred_element_type=F32, precision=lax.Precision.HIGHEST)
    scores = jax.nn.sigmoid(logits)
    sel = scores + rb_ref[...]
    tm = sel.shape[1]
    gsz = n_exp // N_GROUPS
    eid = lax.broadcasted_iota(I32, (n_exp, tm), 0)
    gid_of_e = eid // gsz
    gs = []
    for g in range(N_GROUPS):
        blk = sel[g * gsz:(g + 1) * gsz]
        rid = lax.broadcasted_iota(I32, (gsz, tm), 0)
        m1 = jnp.max(blk, axis=0, keepdims=True)
        a1 = jnp.min(jnp.where(blk == m1, rid, gsz), axis=0, keepdims=True)
        m2 = jnp.max(jnp.where(rid == a1, -jnp.inf, blk), axis=0, keepdims=True)
        gs.append(m1 + m2)
    gs = jnp.concatenate(gs, axis=0)
    grow = lax.broadcasted_iota(I32, (N_GROUPS, tm), 0)
    gm_f = jnp.zeros((N_GROUPS, tm), F32)
    work = gs
    for _ in range(TOPK_GROUPS):
        m = jnp.max(work, axis=0, keepdims=True)
        a = jnp.min(jnp.where(work == m, grow, N_GROUPS), axis=0, keepdims=True)
        hit = grow == a
        gm_f = jnp.where(hit, 1.0, gm_f)
        work = jnp.where(hit, -jnp.inf, work)
    emask = jnp.zeros((n_exp, tm), F32)
    for g in range(N_GROUPS):
        emask = jnp.where(gid_of_e == g, gm_f[g:g + 1], emask)
    work = jnp.where(emask > 0.5, sel, -jnp.inf)
    ids, ws = [], []
    for _ in range(TOP_K):
        m = jnp.max(work, axis=0, keepdims=True)
        a = jnp.min(jnp.where(work == m, eid, n_exp), axis=0, keepdims=True)
        hit = eid == a
        ids.append(a)
        ws.append(jnp.sum(jnp.where(hit, scores, 0.0), axis=0, keepdims=True))
        work = jnp.where(hit, -jnp.inf, work)
    ids = jnp.concatenate(ids, axis=0)
    ws = jnp.concatenate(ws, axis=0)
    idx_ref[...] = ids
    wts_ref[...] = ws / jnp.sum(ws, axis=0, keepdims=True) * ROUTED_SCALE


def ffn_in(x, g, mods, router_wt, router_bias, tm=256):
    t, d = x.shape
    n_exp = router_wt.shape[0]
    row = lambda i: i
    return pl.pallas_call(
        functools.partial(_ffn_in_kernel, mods.tp // tm, n_exp),
        grid=(t // tm,),
        in_specs=[pl.BlockSpec((tm, d), lambda i: (i, 0)), pl.BlockSpec((1, d), lambda i: (0, 0)),
                  *mods.specs(3, tm, row), *mods.specs(4, tm, row),
                  pl.BlockSpec((n_exp, d), lambda i: (0, 0)),
                  pl.BlockSpec((n_exp, 1), lambda i: (0, 0))],
        out_specs=[pl.BlockSpec((tm, d // 2), lambda i: (i, 0)),
                   pl.BlockSpec((TOP_K, tm), lambda i: (0, i)),
                   pl.BlockSpec((TOP_K, tm), lambda i: (0, i))],
        out_shape=[jax.ShapeDtypeStruct((t, d // 2), U32),
                   jax.ShapeDtypeStruct((TOP_K, t), I32),
                   jax.ShapeDtypeStruct((TOP_K, t), F32)],
        compiler_params=_cp("arbitrary"),
        name="ffn_in",
    )(x, g.reshape(1, d), *mods.args(), *mods.args(), router_wt, router_bias.reshape(n_exp, 1))


def _dispatch_kernel(tb, tm, n_exp, n_blocks, slot_ref, tail_ref, used_ref, src_ref, o_hbm,
                     zeros, sem, zsem):
    i = pl.program_id(0)

    def zero_block(b):
        return pltpu.make_async_copy(zeros, o_hbm.at[pl.ds(pl.multiple_of(b * tm, tm), tm)], zsem)

    @pl.when(i == 0)
    def _():
        zeros[...] = jnp.zeros_like(zeros)
        used = used_ref[0]

        def start_tail(e, c):
            zero_block(tail_ref[e]).start()
            return c

        def start_unused(b, c):
            zero_block(b).start()
            return c

        def wait_one(b, c):
            zero_block(0).wait()
            return c

        lax.fori_loop(0, n_exp, start_tail, 0)
        lax.fori_loop(used, n_blocks, start_unused, 0)
        lax.fori_loop(0, n_exp + n_blocks - used, wait_one, 0)

    def issue(r, c):
        tok = i * tb + r
        for k in range(TOP_K):
            pltpu.make_async_copy(src_ref.at[pl.ds(r, 1)],
                                  o_hbm.at[pl.ds(slot_ref[tok * TOP_K + k], 1)], sem).start()
        return c

    lax.fori_loop(0, tb, issue, 0)
    for _ in range(TOP_K):
        pltpu.make_async_copy(src_ref, o_hbm.at[pl.ds(0, tb)], sem).wait()


def dispatch_rows(src, slots_flat, tail_blk, used_blocks, n_slots, tm, tb=128):
    t, w = src.shape
    n_exp = tail_blk.shape[0]
    n_blocks = n_slots // tm
    grid_spec = pltpu.PrefetchScalarGridSpec(
        num_scalar_prefetch=3,
        grid=(t // tb,),
        in_specs=[pl.BlockSpec((tb, w), lambda i, *_: (i, 0))],
        out_specs=pl.BlockSpec(memory_space=pl.ANY),
        scratch_shapes=[pltpu.VMEM((tm, w), src.dtype), pltpu.SemaphoreType.DMA(()),
                        pltpu.SemaphoreType.DMA(())])
    return pl.pallas_call(
        functools.partial(_dispatch_kernel, tb, tm, n_exp, n_blocks),
        grid_spec=grid_spec,
        out_shape=jax.ShapeDtypeStruct((n_slots, w), src.dtype),
        compiler_params=_cp("arbitrary"),
        name="dispatch_rows",
    )(slots_flat, tail_blk, used_blocks, src)


def _stream_group_weights(s, sf_ref, sg_ref, ng_ref, copies, consume):
    @pl.when(sf_ref[s] == 1)
    def _():
        g = sg_ref[s]

        @pl.when(g == 0)
        def _():
            for c in copies(0):
                c.start()

        for c in copies(g):
            c.wait()
        consume()

        @pl.when(g + 1 < ng_ref[0])
        def _():
            for c in copies(g + 1):
                c.start()


def _gate_up_kernel(layer, half, tn, sb_ref, so_ref, sf_ref, sg_ref, nv_ref, ge_ref, gj_ref, ng_ref,
                    x_ref, wg_hbm, wu_hbm, o_ref, stage, wg_s, wu_s, sem):
    s = pl.program_id(0)

    @pl.when(s >= nv_ref[0])
    def _():
        o_ref[...] = jnp.zeros_like(o_ref)

    def copies(g):
        col = pl.ds(pl.multiple_of(gj_ref[g] * tn, tn), tn)
        return (pltpu.make_async_copy(wg_hbm.at[layer, ge_ref[g], :, col], stage.at[0], sem.at[0]),
                pltpu.make_async_copy(wu_hbm.at[layer, ge_ref[g], :, col], stage.at[1], sem.at[0]))

    def consume():
        wg_s[...] = stage[0].astype(BF16)
        wu_s[...] = stage[1].astype(BF16)

    _stream_group_weights(s, sf_ref, sg_ref, ng_ref, copies, consume)

    @pl.when(s < nv_ref[0])
    def _():
        p = x_ref[...]
        lo = _unpack_lo(p).astype(BF16)
        hi = _unpack_hi(p).astype(BF16)
        gt = _dot(lo, wg_s[:half]) + _dot(hi, wg_s[half:])
        up = _dot(lo, wu_s[:half]) + _dot(hi, wu_s[half:])
        o_ref[...] = (_silu(gt) * up).astype(BF16)


def gate_up(x_packed, w_gate, w_up, layer, steps, tm, tn):
    n_slots, half = x_packed.shape
    _, _, d, f = w_gate.shape
    n_steps = steps[0].shape[0]
    grid_spec = pltpu.PrefetchScalarGridSpec(
        num_scalar_prefetch=len(steps),
        grid=(n_steps,),
        in_specs=[pl.BlockSpec((tm, half), lambda s, sb, *_: (sb[s], 0)),
                  pl.BlockSpec(memory_space=pl.ANY), pl.BlockSpec(memory_space=pl.ANY)],
        out_specs=pl.BlockSpec((tm, tn), lambda s, sb, so, *_: (sb[s], so[s])),
        scratch_shapes=[pltpu.VMEM((2, d, tn), F32), pltpu.VMEM((d, tn), BF16),
                        pltpu.VMEM((d, tn), BF16), pltpu.SemaphoreType.DMA((1,))])
    return pl.pallas_call(
        functools.partial(_gate_up_kernel, layer, half, tn),
        grid_spec=grid_spec,
        out_shape=jax.ShapeDtypeStruct((n_slots, f), BF16),
        compiler_params=_cp("arbitrary"),
        name="gate_up",
    )(*steps, x_packed, w_gate, w_up)


def _down_kernel(layer, tn, sb_ref, so_ref, sf_ref, sg_ref, nv_ref, ge_ref, gj_ref, ng_ref,
                 h_ref, w_hbm, o_ref, stage, w_s, sem):
    s = pl.program_id(0)

    @pl.when(s >= nv_ref[0])
    def _():
        o_ref[...] = jnp.zeros_like(o_ref)

    def copies(g):
        col = pl.ds(pl.multiple_of(gj_ref[g] * tn, tn), tn)
        return (pltpu.make_async_copy(w_hbm.at[layer, ge_ref[g], :, col], stage, sem.at[0]),)

    def consume():
        w_s[...] = stage[...].astype(BF16)

    _stream_group_weights(s, sf_ref, sg_ref, ng_ref, copies, consume)

    @pl.when(s < nv_ref[0])
    def _():
        y = _dot(h_ref[...], w_s[...])
        hw = y.shape[1] // 2
        o_ref[...] = _pack_words(y[:, :hw], y[:, hw:])


def down(h1, w_down, layer, steps, tm, tn):
    n_slots, f = h1.shape
    d = w_down.shape[3]
    n_steps = steps[0].shape[0]
    grid_spec = pltpu.PrefetchScalarGridSpec(
        num_scalar_prefetch=len(steps),
        grid=(n_steps,),
        in_specs=[pl.BlockSpec((tm, f), lambda s, sb, *_: (sb[s], 0)),
                  pl.BlockSpec(memory_space=pl.ANY)],
        out_specs=pl.BlockSpec((tm, tn // 2), lambda s, sb, so, *_: (sb[s], so[s])),
        scratch_shapes=[pltpu.VMEM((f, tn), F32), pltpu.VMEM((f, tn), BF16),
                        pltpu.SemaphoreType.DMA((1,))])
    return pl.pallas_call(
        functools.partial(_down_kernel, layer, tn),
        grid_spec=grid_spec,
        out_shape=jax.ShapeDtypeStruct((n_slots, d // 2), U32),
        compiler_params=_cp("arbitrary"),
        name="down",
    )(*steps, h1, w_down)


def _combine_kernel(npb, tb, tn, slot_ref, y_hbm, w_ref, sh_ref, x_ref, gp_ref, gs_ref, o_ref,
                    buf, sem):
    i = pl.program_id(0)
    n = pl.num_programs(0)

    def start_block(blk, slot):
        def body(r, c):
            tok = blk * tb + r
            for k in range(TOP_K):
                pltpu.make_async_copy(y_hbm.at[pl.ds(slot_ref[tok * TOP_K + k], 1)],
                                      buf.at[slot, pl.ds(k * tb + r, 1)], sem.at[slot]).start()
            return c
        lax.fori_loop(0, tb, body, 0)

    @pl.when(i == 0)
    def _():
        start_block(0, 0)

    @pl.when(i + 1 < n)
    def _():
        start_block(i + 1, (i + 1) % 2)

    cur = i % 2
    pltpu.make_async_copy(y_hbm.at[pl.ds(0, TOP_K * tb)], buf.at[cur], sem.at[cur]).wait()

    smp = i >= npb
    gate = lambda c0, c1: jnp.where(smp, gs_ref[:, c0:c1], gp_ref[0, :, c0:c1])
    w = w_ref[...]
    sh = sh_ref
    hw = tn // 2
    cw = min(hw, COMBINE_CHUNK)
    for c0 in range(0, sh_ref.shape[1], cw):
        a = (c0 // hw) * tn + c0 % hw
        lo = _unpack_lo(sh[:, c0:c0 + cw])
        hi = _unpack_hi(sh[:, c0:c0 + cw])
        for k in range(TOP_K):
            p = buf[cur, k * tb:(k + 1) * tb, c0:c0 + cw]
            wk = w[:, k:k + 1]
            lo = lo + wk * _unpack_lo(p)
            hi = hi + wk * _unpack_hi(p)
        o_ref[:, a:a + cw] = x_ref[:, a:a + cw] + gate(a, a + cw) * lo
        o_ref[:, a + hw:a + hw + cw] = (x_ref[:, a + hw:a + hw + cw]
                                        + gate(a + hw, a + hw + cw) * hi)


def combine(y_sorted, slots_flat, wts, y_shared, x, mods, k_gate, tn, tb=64):
    t, d = x.shape
    half = d // 2
    row = lambda i, sl: i
    grid_spec = pltpu.PrefetchScalarGridSpec(
        num_scalar_prefetch=1,
        grid=(t // tb,),
        in_specs=[pl.BlockSpec(memory_space=pl.ANY),
                  pl.BlockSpec((tb, TOP_K), lambda i, sl: (i, 0)),
                  pl.BlockSpec((tb, half), lambda i, sl: (i, 0)),
                  pl.BlockSpec((tb, d), lambda i, sl: (i, 0)),
                  *mods.specs(k_gate, tb, row)],
        out_specs=pl.BlockSpec((tb, d), lambda i, sl: (i, 0)),
        scratch_shapes=[pltpu.VMEM((2, TOP_K * tb, half), U32), pltpu.SemaphoreType.DMA((2,))])
    return pl.pallas_call(
        functools.partial(_combine_kernel, mods.tp // tb, tb, tn),
        grid_spec=grid_spec,
        out_shape=jax.ShapeDtypeStruct((t, d), F32),
        compiler_params=_cp("arbitrary"),
        name="combine",
    )(slots_flat, y_sorted, wts, y_shared, x, *mods.args())


def _dispatch_tables(idx, n_exp, tm, n_tiles):
    t, k = idx.shape
    a = t * k
    flat_e = idx.reshape(a)
    onehot = flat_e[:, None] == jnp.arange(n_exp, dtype=I32)[None, :]
    csum = jnp.cumsum(onehot.astype(I32), axis=0)
    counts = csum[-1]
    nblk = (counts + tm - 1) // tm
    blk_start = jnp.cumsum(nblk) - nblk
    slot = jnp.sum(jnp.where(onehot, (blk_start * tm)[None, :] + csum - 1, 0), axis=1)
    n_blocks = a // tm + n_exp
    n_slots = n_blocks * tm
    tail_blk = (blk_start + jnp.maximum(nblk - 1, 0)).astype(I32)
    used = jnp.sum(nblk).reshape(1).astype(I32)
    steps = [_grouped_steps(nblk, blk_start, n_blocks, nt) for nt in n_tiles]
    return slot.astype(I32), tail_blk, used, steps, n_slots


def _lookup(table, idx):
    n = table.shape[0]
    hit = idx[:, None] == jnp.arange(n, dtype=I32)[None, :]
    return jnp.sum(jnp.where(hit, table[None, :], 0), axis=1)


def _grouped_steps(nblk, blk_start, n_blocks, n_tiles):
    n_exp = nblk.shape[0]
    n_steps = n_blocks * n_tiles
    step_end = jnp.cumsum(nblk) * n_tiles
    total = step_end[-1]
    sidx = jnp.arange(n_steps, dtype=I32)
    valid = sidx < total
    s = jnp.minimum(sidx, total - 1)
    e = jnp.sum(step_end[None, :] <= s[:, None], axis=1).astype(I32)
    nb_e = _lookup(nblk, e)
    r = s - (_lookup(step_end, e) - nb_e * n_tiles)
    j = r // nb_e
    i = r % nb_e
    spare = sidx - total
    blk = jnp.where(valid, _lookup(blk_start, e) + i, jnp.sum(nblk) + spare // n_tiles)
    j_out = jnp.where(valid, j, spare % n_tiles)
    first = jnp.where((i == 0) & valid, 1, 0)
    used = (nblk > 0).astype(I32)
    used_before = jnp.cumsum(used) - used
    grp = _lookup(used_before, e) * n_tiles + j
    n_grp = jnp.sum(used) * n_tiles
    gidx = jnp.arange(n_exp * n_tiles, dtype=I32)
    g_e = jnp.minimum(jnp.sum((used_before + used)[None, :] <= (gidx // n_tiles)[:, None], axis=1),
                      n_exp - 1)
    as_i32 = lambda a: a.astype(I32)
    return (as_i32(blk), as_i32(j_out), as_i32(first), as_i32(grp), as_i32(total.reshape(1)),
            as_i32(g_e), as_i32(gidx % n_tiles), as_i32(n_grp.reshape(1)))


def _dense_steps(n_blocks, n_tiles):
    s = jnp.arange(n_blocks * n_tiles, dtype=I32)
    j = s // n_blocks
    i = s % n_blocks
    g = jnp.arange(n_tiles, dtype=I32)
    return (i, j, (i == 0).astype(I32), j, jnp.full((1,), n_blocks * n_tiles, I32),
            jnp.zeros_like(g), g, jnp.full((1,), n_tiles, I32))


def moe_ffn(x, g, mods, layer, router_wt, router_bias, w_gate, w_up, w_down, sw_gate, sw_up,
            sw_down, tl):
    t, d = x.shape
    _, n_exp, _, f = w_gate.shape
    tm, tn_up, tn_down = tl["moe_tm"], tl["tn_up"], tl["tn_down"]
    hp, idx_t, wts_t = ffn_in(x, g, mods, router_wt, router_bias, tm=tl["ffn_tm"])
    idx = idx_t.T
    wts = wts_t.T
    n_up, n_dn = f // tn_up, d // tn_down
    slot, tail_blk, used, (steps_up, steps_dn), n_slots = _dispatch_tables(idx, n_exp, tm,
                                                                           (n_up, n_dn))
    xs = dispatch_rows(hp, slot, tail_blk, used, n_slots, tm, tb=tl["dispatch_tb"])
    h1 = gate_up(xs, w_gate, w_up, layer, steps_up, tm, tn_up)
    ys = down(h1, w_down, layer, steps_dn, tm, tn_down)
    sh_steps_up = _dense_steps(t // tm, sw_gate.shape[3] // tn_up)
    sh_steps_dn = _dense_steps(t // tm, d // tn_down)
    s1 = gate_up(hp, sw_gate, sw_up, layer, sh_steps_up, tm, tn_up)
    ysh = down(s1, sw_down, layer, sh_steps_dn, tm, tn_down)
    return combine(ys, slot, wts, ysh, x, mods, 5, tn_down, tb=tl["comb_tb"])


def _rope_tables(pos):
    half = QK_ROPE // 2
    inv_freq = ROPE_THETA ** (-jnp.arange(half, dtype=F32) / half)
    ang = pos.astype(F32)[:, None] * inv_freq[None, :]
    z = jnp.zeros((pos.shape[0], LANES - QK_ROPE), F32)
    cos = jnp.concatenate([jnp.cos(ang), jnp.cos(ang), z], axis=1)
    sin = jnp.concatenate([jnp.sin(ang), jnp.sin(ang), z], axis=1)
    return cos, sin


def _rot_cols(w):
    half = w.shape[-1] // 2
    return jnp.concatenate([-w[..., half:], w[..., :half]], axis=-1)


def _pad_cols(w, n):
    return jnp.concatenate([w, jnp.zeros(w.shape[:-1] + (n - w.shape[-1],), w.dtype)], axis=-1)


TILES = dict(adaln_tn=512, norm_tm=256, mla_tm=256, qup_tm=512, kv_tm=512, kv_tn=1024,
             proj_tm=512, proj_tn=1024, tq=1024, tk=1024, n_pg=32, pool_ts=256, pool_seqs=8,
             ffn_tm=256, dispatch_tb=128, moe_tm=256, tn_up=512, tn_down=4096, comb_tb=64)


def kernel(x_prompt, x_sample, cache_kv_latent, cache_k_rope, state_pool, page_table, c_prompt,
           c_sample, ada_w, ada_b, norm_mix_g, norm_ffn_g, mla_w_in, mla_q_norm_g, mla_w_uq,
           mla_kv_norm_g, mla_w_uk, mla_w_uv, mla_w_o, pool_w, pool_scale, router_w, router_bias,
           exp_w_gate, exp_w_up, exp_w_down, shared_w_gate, shared_w_up, shared_w_down,
           final_norm_g):
    return _step(TILES, x_prompt, x_sample, cache_kv_latent, cache_k_rope, state_pool, page_table,
                 c_prompt, c_sample, ada_w, ada_b, norm_mix_g, norm_ffn_g, mla_w_in, mla_q_norm_g,
                 mla_w_uq, mla_kv_norm_g, mla_w_uk, mla_w_uv, mla_w_o, pool_w, pool_scale,
                 router_w, router_bias, exp_w_gate, exp_w_up, exp_w_down, shared_w_gate,
                 shared_w_up, shared_w_down, final_norm_g)


def _step(tl, x_prompt, x_sample, cache_kv_latent, cache_k_rope, state_pool, page_table, c_prompt,
          c_sample, ada_w, ada_b, norm_mix_g, norm_ffn_g, mla_w_in, mla_q_norm_g, mla_w_uq,
          mla_kv_norm_g, mla_w_uk, mla_w_uv, mla_w_o, pool_w, pool_scale, router_w, router_bias,
          exp_w_gate, exp_w_up, exp_w_down, shared_w_gate, shared_w_up, shared_w_down,
          final_norm_g):
    n_p, seq, d = x_prompt.shape
    n_s, dec_seq, _ = x_sample.shape
    depth = ada_w.shape[0]
    tp, ts_tok = n_p * seq, n_s * dec_seq
    t = tp + ts_tok
    page = cache_kv_latent.shape[2]
    past = page_table.shape[1] * page
    q_lora = mla_q_norm_g.shape[1]
    kv_lora = mla_kv_norm_g.shape[1]
    n_h = N_HEADS
    sm_scale = (QK_NOPE + QK_ROPE) ** -0.5
    n_pool_hist = max(POOL_WINDOWS) - 1

    x = None
    x_parts = (x_prompt.reshape(tp, d), x_sample.reshape(ts_tok, d))

    n_p_pad = -(-n_p // 8) * 8
    c_all = jnp.concatenate([c_prompt, jnp.zeros((n_p_pad - n_p, d), F32),
                             jnp.repeat(c_sample, dec_seq, axis=0)], axis=0)
    mod_all = adaln(c_all, ada_w, ada_b, tl["adaln_tn"])

    pos = jnp.concatenate([jnp.tile(jnp.arange(seq), n_p),
                           jnp.tile(past + jnp.arange(dec_seq), n_s)])
    cos_t, sin_t = _rope_tables(pos)

    lat_p, rope_p, pool_p, lat_s, rope_s, pool_s = [], [], [], [], [], []
    for i in range(depth):
        j = i // 2
        mods = Mods(mod_all[i, :n_p].reshape(n_p, 1, 6 * d), mod_all[i, n_p_pad:], seq, tp, d)
        if i % 2 == 0:
            w_in = mla_w_in[j]
            w_r = w_in[:, q_lora + kv_lora:]
            w_aug = jnp.concatenate([w_in[:, :q_lora + kv_lora], _pad_cols(w_r, LANES),
                                     _pad_cols(_rot_cols(w_r), LANES)], axis=1).astype(BF16)
            if x is not None:
                x_parts = (x[:tp], x[tp:])
            cq, ckv, ckv_b, kr, kr_b = mla_in(x_parts, norm_mix_g[i], mods, w_aug, mla_q_norm_g[j],
                                              mla_kv_norm_g[j], cos_t, sin_t, q_lora, kv_lora,
                                              tm=tl["mla_tm"])
            wq = mla_w_uq[j].reshape(q_lora, n_h, QK_NOPE + QK_ROPE)
            wq_r = wq[:, :, QK_NOPE:]
            w1 = jnp.concatenate([wq, jnp.zeros((q_lora, n_h, HEAD_PAD - QK_NOPE - QK_ROPE), F32)],
                                 axis=2).reshape(q_lora, n_h * HEAD_PAD).astype(BF16)
            w2 = _pad_cols(_rot_cols(wq_r), LANES).reshape(q_lora, n_h * LANES).astype(BF16)
            q_cat = q_up(cq, w1, w2, cos_t, sin_t, tm=tl["qup_tm"])

            w_kv = jnp.concatenate([mla_w_uk[j], mla_w_uv[j]], axis=1).astype(BF16)
            kv = matmul(ckv_b[:tp], w_kv, BF16, tl["kv_tm"], tl["kv_tn"])
            attn_p = prompt_attention(q_cat, kv, kr_b, n_p, seq, sm_scale, tl["tq"], tl["tk"])

            w_uk_t = mla_w_uk[j].reshape(kv_lora, n_h, QK_NOPE).transpose(1, 2, 0).astype(BF16)
            w_uv_h = mla_w_uv[j].reshape(kv_lora, n_h, V_HEAD).transpose(1, 0, 2).astype(BF16)
            q_s = q_cat[tp:].reshape(ts_tok, n_h, HEAD_PAD)
            q_nope_s = q_s[:, :, :QK_NOPE].transpose(1, 0, 2)
            q_lat = heads_matmul(
                q_nope_s, w_uk_t,
                pl.BlockSpec((1, ts_tok, QK_NOPE), lambda h: (h, 0, 0)),
                pl.BlockSpec((1, ts_tok, kv_lora), lambda h: (h, 0, 0)),
                jax.ShapeDtypeStruct((n_h, ts_tok, kv_lora), BF16))
            q_lat = q_lat.transpose(1, 0, 2).reshape(n_s, dec_seq * n_h, kv_lora)
            q_rope_s = q_s[:, :, QK_NOPE:QK_NOPE + QK_ROPE].reshape(n_s, dec_seq * n_h, QK_ROPE)
            kpad = page - dec_seq
            new_lat = jnp.pad(ckv_b[tp:].reshape(n_s, dec_seq, kv_lora), ((0, 0), (0, kpad), (0, 0)))
            new_rope = jnp.pad(kr_b[tp:, :QK_ROPE].reshape(n_s, dec_seq, QK_ROPE),
                               ((0, 0), (0, kpad), (0, 0))).swapaxes(1, 2)
            o_lat = sample_attention(q_lat, q_rope_s, cache_kv_latent,
                                     cache_k_rope.swapaxes(2, 3), j,
                                     page_table, new_lat, new_rope, dec_seq, sm_scale,
                                     n_pg=tl["n_pg"])
            o_lat = o_lat.reshape(ts_tok, n_h, kv_lora).transpose(1, 0, 2)
            attn_s = heads_matmul(
                o_lat, w_uv_h,
                pl.BlockSpec((1, ts_tok, kv_lora), lambda h: (h, 0, 0)),
                pl.BlockSpec((ts_tok, V_HEAD), lambda h: (0, h)),
                jax.ShapeDtypeStruct((ts_tok, n_h * V_HEAD), BF16))
            x = proj_residual((attn_p, attn_s), mla_w_o[j].astype(BF16), x_parts, mods, 2,
                              tl["proj_tm"], tl["proj_tn"])

            lat_p.append(ckv[:tp].reshape(n_p, seq, kv_lora))
            rope_p.append(kr[:tp, :QK_ROPE].reshape(n_p, seq, QK_ROPE))
            lat_s.append(ckv[tp:].reshape(n_s, dec_seq, kv_lora))
            rope_s.append(kr[tp:, :QK_ROPE].reshape(n_s, dec_seq, QK_ROPE))
        else:
            if x is None:
                x = jnp.concatenate(x_parts, axis=0)
            h = modnorm(x, norm_mix_g[i], mods, 0, 1, F32, tm=tl["norm_tm"])
            pw = pool_w[j].astype(BF16)
            gate_p = mods.mp[:, :, 2 * d:3 * d]
            xs3 = x[tp:].reshape(n_s, dec_seq, d)
            x_new = pool_mixer(h, x, gate_p, pw, pool_scale[j], n_p, seq, 0, tl["pool_ts"],
                               in_place=True)
            hs3 = h[tp:].reshape(n_s, dec_seq, d)
            rows = POOL_HALO + 8
            tail = rows - POOL_HALO - dec_seq
            lead = POOL_HALO - n_pool_hist
            hh = jnp.concatenate([jnp.zeros((n_s, lead, d), F32), state_pool[j], hs3,
                                  jnp.zeros((n_s, tail, d), F32)], axis=1)
            pad3 = lambda a: jnp.pad(a, ((0, 0), (POOL_HALO, tail), (0, 0)))
            gate_s = mods.ms[:, 2 * d:3 * d].reshape(n_s, dec_seq, d)
            seqs = tl["pool_seqs"]
            xs_new = pool_mixer(hh.reshape(n_s * rows, d), pad3(xs3).reshape(n_s * rows, d),
                                pad3(gate_s).reshape(n_s * rows, d), pw, pool_scale[j],
                                n_s // seqs, seqs * rows, past - POOL_HALO, seqs * rows, period=rows)
            xs_new = xs_new.reshape(n_s, rows, d)[:, POOL_HALO:POOL_HALO + dec_seq]
            x = lax.dynamic_update_slice(x_new, xs_new.reshape(ts_tok, d), (tp, 0))
            pool_p.append(jnp.stack([h[(b + 1) * seq - n_pool_hist:(b + 1) * seq]
                                     for b in range(n_p)]))
            pool_s.append(jnp.concatenate([state_pool[j], hs3], axis=1)[:, dec_seq:])

        x = moe_ffn(x, norm_ffn_g[i], mods, i, router_w[i].T, router_bias[i], exp_w_gate,
                    exp_w_up, exp_w_down, shared_w_gate[:, None], shared_w_up[:, None],
                    shared_w_down[:, None], tl)

    y_p, y_s = rmsnorm_split(x, final_norm_g, tp, tm=tl["norm_tm"])
    return (y_p.reshape(n_p, seq, d), y_s.reshape(n_s, dec_seq, d),
            jnp.stack(lat_p), jnp.stack(rope_p), jnp.stack(pool_p),
            jnp.stack(lat_s), jnp.stack(rope_s), jnp.stack(pool_s))
```

```python
import functools

import jax
import jax.numpy as jnp
from jax import lax
from jax.experimental import pallas as pl
from jax.experimental.pallas import tpu as pltpu

F32 = jnp.float32
BF16 = jnp.bfloat16
U32 = jnp.uint32
I32 = jnp.int32

EPS = 1e-6
ROPE_THETA = 10000.0
ROUTED_SCALE = 2.5
N_GROUPS = 8
TOPK_GROUPS = 4
TOP_K = 8
POOL_WINDOWS = (2, 4, 8, 16)
POOL_HALO = 16
N_HEADS = 32
QK_NOPE = 128
QK_ROPE = 64
V_HEAD = 128
LANES = 128
HEAD_PAD = 2 * LANES
VMEM_LIMIT = 56 * 1024 * 1024
COMBINE_CHUNK = 256
NEG_BIG = -1e30
LOG2_E = 1.4426950408889634


def _cp(*sem, vmem=VMEM_LIMIT):
    return pltpu.CompilerParams(dimension_semantics=sem, vmem_limit_bytes=vmem)


def _dot(a, b):
    return jnp.dot(a, b, preferred_element_type=F32)


def _dot_nt(a, b):
    return lax.dot_general(a, b, (((1,), (1,)), ((), ())), preferred_element_type=F32)


def _rms(x, g):
    return x * lax.rsqrt(jnp.mean(x * x, axis=-1, keepdims=True) + EPS) * g


def _silu(x):
    return x * jax.nn.sigmoid(x)


def _unpack_lo(p):
    return lax.bitcast_convert_type(p << 16, F32)


def _unpack_hi(p):
    return lax.bitcast_convert_type(p & jnp.uint32(0xFFFF0000), F32)


def _pack_words(lo, hi):
    lo_b = lax.bitcast_convert_type(lo.astype(BF16).astype(F32), U32)
    hi_b = lax.bitcast_convert_type(hi.astype(BF16).astype(F32), U32)
    return (hi_b & jnp.uint32(0xFFFF0000)) | (lo_b >> 16)


def _adaln_kernel(c_ref, w_ref, b_ref, o_ref, s_ref):
    @pl.when(jnp.logical_and(pl.program_id(0) == 0, pl.program_id(1) == 0))
    def _():
        s_ref[...] = _silu(c_ref[...]).astype(BF16)

    o_ref[0] = _dot(s_ref[...], w_ref[0].astype(BF16)) + b_ref[0]


def adaln(c_all, ada_w, ada_b, tn=512):
    depth, d, n = ada_w.shape
    r = c_all.shape[0]
    return pl.pallas_call(
        _adaln_kernel,
        grid=(depth, n // tn),
        in_specs=[pl.BlockSpec((r, d), lambda l, j: (0, 0)),
                  pl.BlockSpec((1, d, tn), lambda l, j: (l, 0, j)),
                  pl.BlockSpec((1, 1, tn), lambda l, j: (l, 0, j))],
        out_specs=pl.BlockSpec((1, r, tn), lambda l, j: (l, 0, j)),
        out_shape=jax.ShapeDtypeStruct((depth, r, n), F32),
        scratch_shapes=[pltpu.VMEM((r, d), BF16)],
        compiler_params=_cp("arbitrary", "arbitrary"),
        name="adaln",
    )(c_all, ada_w, ada_b.reshape(depth, 1, n))


class Mods:
    def __init__(self, mp, ms, seq, n_prompt_tok, d):
        self.mp, self.ms, self.seq, self.tp, self.d = mp, ms, seq, n_prompt_tok, d

    def specs(self, k, tm, row_of):
        npb = self.tp // tm
        per_seq = self.seq // tm
        n_p = self.mp.shape[0]

        def p_map(*g):
            return (jnp.minimum(row_of(*g) // per_seq, n_p - 1), 0, k)

        def s_map(*g):
            return (jnp.maximum(row_of(*g) - npb, 0), k)

        return [pl.BlockSpec((1, 1, self.d), p_map), pl.BlockSpec((tm, self.d), s_map)]

    def args(self):
        return [self.mp, self.ms]


def _pick(is_sample, p_ref, s_ref):
    return jnp.where(is_sample, s_ref[...], p_ref[0])


def _modnorm_kernel(npb, x_ref, g_ref, shp_ref, shs_ref, scp_ref, scs_ref, o_ref):
    smp = pl.program_id(0) >= npb
    shift = _pick(smp, shp_ref, shs_ref)
    scale = _pick(smp, scp_ref, scs_ref)
    o_ref[...] = (_rms(x_ref[...], g_ref[...]) * (1.0 + scale) + shift).astype(o_ref.dtype)


def modnorm(x, g, mods, k_shift, k_scale, out_dtype, tm=256):
    t, d = x.shape
    row = lambda i: i
    return pl.pallas_call(
        functools.partial(_modnorm_kernel, mods.tp // tm),
        grid=(t // tm,),
        in_specs=[pl.BlockSpec((tm, d), lambda i: (i, 0)),
                  pl.BlockSpec((1, d), lambda i: (0, 0)),
                  *mods.specs(k_shift, tm, row), *mods.specs(k_scale, tm, row)],
        out_specs=pl.BlockSpec((tm, d), lambda i: (i, 0)),
        out_shape=jax.ShapeDtypeStruct((t, d), out_dtype),
        compiler_params=_cp("arbitrary"),
        name="modnorm",
    )(x, g.reshape(1, d), *mods.args(), *mods.args())


def _rmsnorm_kernel(npb, x_ref, g_ref, op_ref, os_ref):
    i = pl.program_id(0)
    y = _rms(x_ref[...], g_ref[...])

    @pl.when(i < npb)
    def _():
        op_ref[...] = y

    @pl.when(i >= npb)
    def _():
        os_ref[...] = y


def rmsnorm_split(x, g, tp, tm=256):
    t, d = x.shape
    npb = tp // tm
    return pl.pallas_call(
        functools.partial(_rmsnorm_kernel, npb),
        grid=(t // tm,),
        in_specs=[pl.BlockSpec((tm, d), lambda i: (i, 0)), pl.BlockSpec((1, d), lambda i: (0, 0))],
        out_specs=[pl.BlockSpec((tm, d), lambda i: (jnp.minimum(i, npb - 1), 0)),
                   pl.BlockSpec((tm, d), lambda i: (jnp.maximum(i - npb, 0), 0))],
        out_shape=[jax.ShapeDtypeStruct((tp, d), F32), jax.ShapeDtypeStruct((t - tp, d), F32)],
        compiler_params=_cp("arbitrary"),
        name="final_rmsnorm",
    )(x, g.reshape(1, d))


def _split_row_specs(parts, tm, width, npb, row_of, col_of=lambda *g: 0):
    def p_map(*g):
        return (jnp.minimum(row_of(*g), npb - 1), col_of(*g))

    def s_map(*g):
        return (jnp.maximum(row_of(*g) - npb, 0), col_of(*g))

    return [pl.BlockSpec((tm, width), p_map), pl.BlockSpec((tm, width), s_map)]


def _pick_rows(is_sample, p_ref, s_ref):
    return jnp.where(is_sample, s_ref[...], p_ref[...])


def _mla_in_kernel(npb, q_lora, kv_lora, xp_ref, xs_ref, g_ref, shp_ref, shs_ref, scp_ref, scs_ref,
                   w_ref, gq_ref, gkv_ref, cos_ref, sin_ref,
                   cq_ref, ckv_ref, ckvb_ref, kr_ref, krb_ref):
    smp = pl.program_id(0) >= npb
    shift = _pick(smp, shp_ref, shs_ref)
    scale = _pick(smp, scp_ref, scs_ref)
    x = _pick_rows(smp, xp_ref, xs_ref)
    h = (_rms(x, g_ref[...]) * (1.0 + scale) + shift).astype(BF16)
    a = _dot(h, w_ref[...])
    cq_ref[...] = _rms(a[:, :q_lora], gq_ref[...]).astype(BF16)
    ckv = _rms(a[:, q_lora:q_lora + kv_lora], gkv_ref[...])
    ckv_ref[...] = ckv
    ckvb_ref[...] = ckv.astype(BF16)
    o = q_lora + kv_lora
    kr = a[:, o:o + LANES] * cos_ref[...] + a[:, o + LANES:o + 2 * LANES] * sin_ref[...]
    kr_ref[...] = kr
    krb_ref[...] = kr.astype(BF16)


def mla_in(x_parts, g, mods, w_aug, gq, gkv, cos_t, sin_t, q_lora, kv_lora, tm=256):
    d = x_parts[0].shape[1]
    t = x_parts[0].shape[0] + x_parts[1].shape[0]
    n_aug = w_aug.shape[1]
    row = lambda i: i
    blk = lambda w: pl.BlockSpec((tm, w), lambda i: (i, 0))
    return pl.pallas_call(
        functools.partial(_mla_in_kernel, mods.tp // tm, q_lora, kv_lora),
        grid=(t // tm,),
        in_specs=[*_split_row_specs(x_parts, tm, d, mods.tp // tm, row),
                  pl.BlockSpec((1, d), lambda i: (0, 0)),
                  *mods.specs(0, tm, row), *mods.specs(1, tm, row),
                  pl.BlockSpec((d, n_aug), lambda i: (0, 0)),
                  pl.BlockSpec((1, q_lora), lambda i: (0, 0)),
                  pl.BlockSpec((1, kv_lora), lambda i: (0, 0)),
                  blk(LANES), blk(LANES)],
        out_specs=[blk(q_lora), blk(kv_lora), blk(kv_lora), blk(LANES), blk(LANES)],
        out_shape=[jax.ShapeDtypeStruct((t, q_lora), BF16),
                   jax.ShapeDtypeStruct((t, kv_lora), F32),
                   jax.ShapeDtypeStruct((t, kv_lora), BF16),
                   jax.ShapeDtypeStruct((t, LANES), F32),
                   jax.ShapeDtypeStruct((t, LANES), BF16)],
        compiler_params=_cp("arbitrary"),
        name="mla_in",
    )(*x_parts, g.reshape(1, d), *mods.args(), *mods.args(), w_aug, gq.reshape(1, -1),
      gkv.reshape(1, -1), cos_t, sin_t)


def _q_up_kernel(hg, cq_ref, w1_ref, w2_ref, cos_ref, sin_ref, o_ref):
    cq = cq_ref[...]
    a = _dot(cq, w1_ref[...])
    r = _dot(cq, w2_ref[...])
    c = cos_ref[...]
    s = sin_ref[...]
    for h in range(hg):
        b = h * HEAD_PAD
        o_ref[:, b:b + LANES] = a[:, b:b + LANES].astype(BF16)
        o_ref[:, b + LANES:b + HEAD_PAD] = (
            a[:, b + LANES:b + HEAD_PAD] * c + r[:, h * LANES:(h + 1) * LANES] * s).astype(BF16)


def q_up(cq, w1, w2, cos_t, sin_t, tm=512, hg=4):
    t, k = cq.shape
    n_h = w1.shape[1] // HEAD_PAD
    return pl.pallas_call(
        functools.partial(_q_up_kernel, hg),
        grid=(n_h // hg, t // tm),
        in_specs=[pl.BlockSpec((tm, k), lambda j, i: (i, 0)),
                  pl.BlockSpec((k, hg * HEAD_PAD), lambda j, i: (0, j)),
                  pl.BlockSpec((k, hg * LANES), lambda j, i: (0, j)),
                  pl.BlockSpec((tm, LANES), lambda j, i: (i, 0)),
                  pl.BlockSpec((tm, LANES), lambda j, i: (i, 0))],
        out_specs=pl.BlockSpec((tm, hg * HEAD_PAD), lambda j, i: (i, j)),
        out_shape=jax.ShapeDtypeStruct((t, n_h * HEAD_PAD), BF16),
        compiler_params=_cp("arbitrary", "arbitrary"),
        name="q_up",
    )(cq, w1, w2, cos_t, sin_t)


def _mm_kernel(a_ref, b_ref, o_ref):
    o_ref[...] = _dot(a_ref[...], b_ref[...]).astype(o_ref.dtype)


def matmul(a, b, out_dtype, tm=512, tn=1024):
    m, k = a.shape
    n = b.shape[1]
    tm, tn = min(tm, m), min(tn, n)
    return pl.pallas_call(
        _mm_kernel,
        grid=(n // tn, m // tm),
        in_specs=[pl.BlockSpec((tm, k), lambda j, i: (i, 0)),
                  pl.BlockSpec((k, tn), lambda j, i: (0, j))],
        out_specs=pl.BlockSpec((tm, tn), lambda j, i: (i, j)),
        out_shape=jax.ShapeDtypeStruct((m, n), out_dtype),
        compiler_params=_cp("arbitrary", "arbitrary"),
        name="matmul",
    )(a, b)


def _proj_res_kernel(npb, ap_ref, as_ref, w_ref, xp_ref, xs_ref, gp_ref, gs_ref, o_ref):
    smp = pl.program_id(1) >= npb
    gate = _pick(smp, gp_ref, gs_ref)
    a = _pick_rows(smp, ap_ref, as_ref)
    o_ref[...] = _pick_rows(smp, xp_ref, xs_ref) + gate * _dot(a, w_ref[...])


def proj_residual(a_parts, w, x_parts, mods, k_gate, tm=256, tn=512):
    k = a_parts[0].shape[1]
    t = a_parts[0].shape[0] + a_parts[1].shape[0]
    d = w.shape[1]
    npb = mods.tp // tm
    per_seq = mods.seq // tm
    n_p = mods.mp.shape[0]
    cb = mods.d // tn
    gp = pl.BlockSpec((1, 1, tn), lambda j, i: (jnp.minimum(i // per_seq, n_p - 1), 0, k_gate * cb + j))
    gs = pl.BlockSpec((tm, tn), lambda j, i: (jnp.maximum(i - npb, 0), k_gate * cb + j))
    return pl.pallas_call(
        functools.partial(_proj_res_kernel, npb),
        grid=(d // tn, t // tm),
        in_specs=[*_split_row_specs(a_parts, tm, k, npb, lambda j, i: i),
                  pl.BlockSpec((k, tn), lambda j, i: (0, j)),
                  *_split_row_specs(x_parts, tm, tn, npb, lambda j, i: i, lambda j, i: j), gp, gs],
        out_specs=pl.BlockSpec((tm, tn), lambda j, i: (i, j)),
        out_shape=jax.ShapeDtypeStruct((t, d), F32),
        compiler_params=_cp("arbitrary", "arbitrary"),
        name="proj_residual",
    )(*a_parts, w, *x_parts, *mods.args())


def _flash_kernel(tq, tk, scale, q_ref, kn_ref, kr_ref, v_ref, o_ref, m_s, l_s, acc_s):
    qi = pl.program_id(2)
    ki = pl.program_id(3)
    nk = pl.num_programs(3)

    @pl.when(ki == 0)
    def _():
        m_s[...] = jnp.full_like(m_s, NEG_BIG)
        l_s[...] = jnp.zeros_like(l_s)
        acc_s[...] = jnp.zeros_like(acc_s)

    def update(masked):
        k = jnp.concatenate([kn_ref[...], kr_ref[...]], axis=1)
        s = _dot_nt(q_ref[...], k) * (scale * LOG2_E)
        if masked:
            qpos = qi * tq + lax.broadcasted_iota(I32, (tq, tk), 0)
            kpos = ki * tk + lax.broadcasted_iota(I32, (tq, tk), 1)
            s = jnp.where(kpos <= qpos, s, NEG_BIG)
        m_old = m_s[...]
        m_new = jnp.maximum(m_old, jnp.max(s, axis=-1, keepdims=True))
        alpha = jnp.exp2(m_old - m_new)
        p = jnp.exp2(s - m_new)
        l_s[...] = alpha * l_s[...] + jnp.sum(p, axis=-1, keepdims=True)
        acc_s[...] = alpha * acc_s[...] + _dot(p.astype(BF16), v_ref[...])
        m_s[...] = m_new

    live = ki * tk <= qi * tq + (tq - 1)
    crosses = ki * tk + (tk - 1) > qi * tq

    @pl.when(jnp.logical_and(live, crosses))
    def _():
        update(True)

    @pl.when(jnp.logical_and(live, jnp.logical_not(crosses)))
    def _():
        update(False)

    @pl.when(ki == nk - 1)
    def _():
        o_ref[...] = (acc_s[...] / l_s[...]).astype(o_ref.dtype)


def prompt_attention(q_cat, kv, kr_b, n_seq, seq, scale, tq=1024, tk=1024):
    n_h = q_cat.shape[1] // HEAD_PAD
    tq, tk = min(tq, seq), min(tk, seq)
    nq, nk = seq // tq, seq // tk

    def kmap(b, h, qi, ki):
        return jnp.minimum(ki, (qi * tq + tq - 1) // tk)

    return pl.pallas_call(
        functools.partial(_flash_kernel, tq, tk, scale),
        grid=(n_seq, n_h, nq, nk),
        in_specs=[pl.BlockSpec((tq, HEAD_PAD), lambda b, h, qi, ki: (b * nq + qi, h)),
                  pl.BlockSpec((tk, QK_NOPE), lambda b, h, qi, ki: (b * nk + kmap(b, h, qi, ki), h)),
                  pl.BlockSpec((tk, LANES), lambda b, h, qi, ki: (b * nk + kmap(b, h, qi, ki), 0)),
                  pl.BlockSpec((tk, V_HEAD), lambda b, h, qi, ki: (b * nk + kmap(b, h, qi, ki), n_h + h))],
        out_specs=pl.BlockSpec((tq, V_HEAD), lambda b, h, qi, ki: (b * nq + qi, h)),
        out_shape=jax.ShapeDtypeStruct((n_seq * seq, n_h * V_HEAD), BF16),
        scratch_shapes=[pltpu.VMEM((tq, 1), F32), pltpu.VMEM((tq, 1), F32),
                        pltpu.VMEM((tq, V_HEAD), F32)],
        compiler_params=_cp("arbitrary", "arbitrary", "arbitrary", "arbitrary"),
        name="prompt_attention",
    )(q_cat, kv, kr_b, kv)


def _bmm_kernel(a_ref, b_ref, o_ref):
    a = a_ref[...]
    a = a.reshape(a.shape[-2:])
    o = _dot(a, b_ref[0]).astype(o_ref.dtype)
    o_ref[...] = o.reshape(o_ref.shape)


def heads_matmul(a, b, a_spec, out_spec, out_shape):
    n_h = b.shape[0]
    return pl.pallas_call(
        _bmm_kernel,
        grid=(n_h,),
        in_specs=[a_spec, pl.BlockSpec((1,) + b.shape[1:], lambda h: (h, 0, 0))],
        out_specs=out_spec,
        out_shape=out_shape,
        compiler_params=_cp("arbitrary"),
        name="heads_matmul",
    )(a, b)


def _decode_kernel(layer, n_pg, n_grp, dec_seq, scale, pt_ref, ql_ref, qr_ref, lat_hbm, rope_hbm,
                   nlat_ref, nrope_ref, o_ref, lat_buf, rope_buf, sem, m_s, l_s, acc_s):
    i = pl.program_id(0)
    g = pl.program_id(1)
    n_seq = pl.num_programs(0)
    page = lat_hbm.shape[2]

    def copies(seq, grp, slot):
        out = []
        for k in range(n_pg):
            pg = pt_ref[seq, grp * n_pg + k]
            out.append(pltpu.make_async_copy(lat_hbm.at[layer, pg],
                                             lat_buf.at[slot, pl.ds(k * page, page)], sem.at[slot]))
            out.append(pltpu.make_async_copy(rope_hbm.at[layer, pg],
                                             rope_buf.at[slot, :, pl.ds(k * page, page)],
                                             sem.at[slot]))
        return out

    step = i * n_grp + g
    slot = step % 2

    @pl.when(jnp.logical_and(i == 0, g == 0))
    def _():
        for c in copies(0, 0, 0):
            c.start()

    @pl.when(jnp.logical_and(g < n_grp, step + 1 < n_seq * n_grp))
    def _():
        nxt = step + 1
        for c in copies(nxt // n_grp, nxt % n_grp, nxt % 2):
            c.start()

    @pl.when(g == 0)
    def _():
        m_s[...] = jnp.full_like(m_s, NEG_BIG)
        l_s[...] = jnp.zeros_like(l_s)
        acc_s[...] = jnp.zeros_like(acc_s)

    def update(lat, rope_t, mask):
        s = (_dot_nt(ql_ref[0], lat) + _dot(qr_ref[0], rope_t)) * (scale * LOG2_E)
        if mask is not None:
            s = jnp.where(mask, s, NEG_BIG)
        m_old = m_s[...]
        m_new = jnp.maximum(m_old, jnp.max(s, axis=-1, keepdims=True))
        alpha = jnp.exp2(m_old - m_new)
        p = jnp.exp2(s - m_new)
        l_s[...] = alpha * l_s[...] + jnp.sum(p, axis=-1, keepdims=True)
        acc_s[...] = alpha * acc_s[...] + _dot(p.astype(BF16), lat)
        m_s[...] = m_new

    @pl.when(g < n_grp)
    def _():
        for c in copies(i, g, slot):
            c.wait()
        update(lat_buf[slot].astype(BF16), rope_buf[slot].astype(BF16), None)

    @pl.when(g == n_grp)
    def _():
        lat = nlat_ref[0]
        rows, keys = ql_ref.shape[1], lat.shape[0]
        q_tok = lax.broadcasted_iota(I32, (rows, keys), 0) // (rows // dec_seq)
        k_tok = lax.broadcasted_iota(I32, (rows, keys), 1)
        update(lat, nrope_ref[0], k_tok <= q_tok)
        o_ref[0] = (acc_s[...] / l_s[...]).astype(o_ref.dtype)


def sample_attention(q_lat, q_rope, cache_lat, cache_rope, layer, page_table, new_lat, new_rope,
                     dec_seq, scale, n_pg=16):
    n, r, c = q_lat.shape
    page = cache_lat.shape[2]
    rd = cache_rope.shape[2]
    n_pages = page_table.shape[1]
    n_pg = min(n_pg, n_pages)
    n_grp = n_pages // n_pg
    kn = new_lat.shape[1]

    seq_map = lambda i, g, pt: (i, 0, 0)
    in_specs = [pl.BlockSpec((1, r, c), seq_map), pl.BlockSpec((1, r, rd), seq_map),
                pl.BlockSpec(memory_space=pl.ANY), pl.BlockSpec(memory_space=pl.ANY),
                pl.BlockSpec((1, kn, c), seq_map), pl.BlockSpec((1, rd, kn), seq_map)]
    grid_spec = pltpu.PrefetchScalarGridSpec(
        num_scalar_prefetch=1,
        grid=(n, n_grp + 1),
        in_specs=in_specs,
        out_specs=pl.BlockSpec((1, r, c), seq_map),
        scratch_shapes=[pltpu.VMEM((2, n_pg * page, c), F32), pltpu.VMEM((2, rd, n_pg * page), F32),
                        pltpu.SemaphoreType.DMA((2,)),
                        pltpu.VMEM((r, 1), F32), pltpu.VMEM((r, 1), F32), pltpu.VMEM((r, c), F32)])
    return pl.pallas_call(
        functools.partial(_decode_kernel, layer, n_pg, n_grp, dec_seq, scale),
        grid_spec=grid_spec,
        out_shape=jax.ShapeDtypeStruct((n, r, c), BF16),
        compiler_params=_cp("arbitrary", "arbitrary"),
        name="sample_attention",
    )(page_table, q_lat, q_rope, cache_lat, cache_rope, new_lat, new_rope)


def _pool_kernel(ts, grp_w, first_pos, period, gate_per_seq, h_ref, halo_ref, x_ref, gate_ref, w_ref,
                 ps_ref, o_ref):
    i = pl.program_id(1)
    keep = jnp.where(i > 0, 1.0, 0.0).astype(F32)
    u_all = jnp.concatenate([halo_ref[...] * keep, h_ref[...]], axis=0)
    rows = POOL_HALO + ts
    r = i * ts - POOL_HALO + lax.broadcasted_iota(I32, (rows, 1), 0)
    if period:
        r = (r + period) % period
    pos = first_pos + r
    for g, win in enumerate(POOL_WINDOWS):
        u = u_all[:, g * grp_w:(g + 1) * grp_w]
        s = u
        k = 1
        while k < win:
            s = s + pltpu.roll(s, k, axis=0)
            k *= 2
        count = jnp.clip(pos + 1, 1, win).astype(F32)
        dlt = (s / count - u)[POOL_HALO:].astype(BF16)
        cols = slice(g * grp_w, (g + 1) * grp_w)
        y = _dot(dlt, w_ref[g]) * ps_ref[:, cols]
        gate = gate_ref[0, :, cols] if gate_per_seq else gate_ref[:, cols]
        o_ref[:, cols] = x_ref[:, cols] + gate * y


def pool_mixer(h, x, gate, pool_w, pool_scale, n_seq, seq_len, first_pos, ts, period=0,
               in_place=False):
    d = h.shape[1]
    grp_w = d // len(POOL_WINDOWS)
    nb = seq_len // ts
    hb = ts // POOL_HALO
    row_map = lambda b, i: (b * nb + i, 0)
    gate_per_seq = gate.ndim == 3
    gate_spec = (pl.BlockSpec((1, 1, d), lambda b, i: (b, 0, 0)) if gate_per_seq
                 else pl.BlockSpec((ts, d), row_map))
    return pl.pallas_call(
        functools.partial(_pool_kernel, ts, grp_w, first_pos, period, gate_per_seq),
        grid=(n_seq, nb),
        in_specs=[pl.BlockSpec((ts, d), row_map),
                  pl.BlockSpec((POOL_HALO, d), lambda b, i: (jnp.maximum((b * nb + i) * hb - 1, 0), 0)),
                  pl.BlockSpec((ts, d), row_map),
                  gate_spec,
                  pl.BlockSpec(pool_w.shape, lambda b, i: (0, 0, 0)),
                  pl.BlockSpec((1, d), lambda b, i: (0, 0))],
        out_specs=pl.BlockSpec((ts, d), row_map),
        out_shape=jax.ShapeDtypeStruct(x.shape, F32),
        input_output_aliases={2: 0} if in_place else {},
        compiler_params=_cp("arbitrary", "arbitrary"),
        name="pool_mixer",
    )(h, h, x, gate, pool_w, pool_scale.reshape(1, d))


def _ffn_in_kernel(npb, n_exp, x_ref, g_ref, shp_ref, shs_ref, scp_ref, scs_ref, rw_ref, rb_ref,
                   h_ref, idx_ref, wts_ref):
    smp = pl.program_id(0) >= npb
    shift = _pick(smp, shp_ref, shs_ref)
    scale = _pick(smp, scp_ref, scs_ref)
    h = _rms(x_ref[...], g_ref[...]) * (1.0 + scale) + shift
    half = h.shape[1] // 2
    h_ref[...] = _pack_words(h[:, :half], h[:, half:])
    logits = lax.dot_general(rw_ref[...], h, (((1,), (1,)), ((), ())),
                             preferred_element_type=F32, precision=lax.Precision.HIGHEST)
    scores = jax.nn.sigmoid(logits)
    sel = scores + rb_ref[...]
    tm = sel.shape[1]
    gsz = n_exp // N_GROUPS
    eid = lax.broadcasted_iota(I32, (n_exp, tm), 0)
    gid_of_e = eid // gsz
    gs = []
    for g in range(N_GROUPS):
        blk = sel[g * gsz:(g + 1) * gsz]
        rid = lax.broadcasted_iota(I32, (gsz, tm), 0)
        m1 = jnp.max(blk, axis=0, keepdims=True)
        a1 = jnp.min(jnp.where(blk == m1, rid, gsz), axis=0, keepdims=True)
        m2 = jnp.max(jnp.where(rid == a1, -jnp.inf, blk), axis=0, keepdims=True)
        gs.append(m1 + m2)
    gs = jnp.concatenate(gs, axis=0)
    grow = lax.broadcasted_iota(I32, (N_GROUPS, tm), 0)
    gm_f = jnp.zeros((N_GROUPS, tm), F32)
    work = gs
    for _ in range(TOPK_GROUPS):
        m = jnp.max(work, axis=0, keepdims=True)
        a = jnp.min(jnp.where(work == m, grow, N_GROUPS), axis=0, keepdims=True)
        hit = grow == a
        gm_f = jnp.where(hit, 1.0, gm_f)
        work = jnp.where(hit, -jnp.inf, work)
    emask = jnp.zeros((n_exp, tm), F32)
    for g in range(N_GROUPS):
        emask = jnp.where(gid_of_e == g, gm_f[g:g + 1], emask)
    work = jnp.where(emask > 0.5, sel, -jnp.inf)
    ids, ws = [], []
    for _ in range(TOP_K):
        m = jnp.max(work, axis=0, keepdims=True)
        a = jnp.min(jnp.where(work == m, eid, n_exp), axis=0, keepdims=True)
        hit = eid == a
        ids.append(a)
        ws.append(jnp.sum(jnp.where(hit, scores, 0.0), axis=0, keepdims=True))
        work = jnp.where(hit, -jnp.inf, work)
    ids = jnp.concatenate(ids, axis=0)
    ws = jnp.concatenate(ws, axis=0)
    idx_ref[...] = ids
    wts_ref[...] = ws / jnp.sum(ws, axis=0, keepdims=True) * ROUTED_SCALE


def ffn_in(x, g, mods, router_wt, router_bias, tm=256):
    t, d = x.shape
    n_exp = router_wt.shape[0]
    row = lambda i: i
    return pl.pallas_call(
        functools.partial(_ffn_in_kernel, mods.tp // tm, n_exp),
        grid=(t // tm,),
        in_specs=[pl.BlockSpec((tm, d), lambda i: (i, 0)), pl.BlockSpec((1, d), lambda i: (0, 0)),
                  *mods.specs(3, tm, row), *mods.specs(4, tm, row),
                  pl.BlockSpec((n_exp, d), lambda i: (0, 0)),
                  pl.BlockSpec((n_exp, 1), lambda i: (0, 0))],
        out_specs=[pl.BlockSpec((tm, d // 2), lambda i: (i, 0)),
                   pl.BlockSpec((TOP_K, tm), lambda i: (0, i)),
                   pl.BlockSpec((TOP_K, tm), lambda i: (0, i))],
        out_shape=[jax.ShapeDtypeStruct((t, d // 2), U32),
                   jax.ShapeDtypeStruct((TOP_K, t), I32),
                   jax.ShapeDtypeStruct((TOP_K, t), F32)],
        compiler_params=_cp("arbitrary"),
        name="ffn_in",
    )(x, g.reshape(1, d), *mods.args(), *mods.args(), router_wt, router_bias.reshape(n_exp, 1))


def _dispatch_kernel(tb, tm, n_exp, n_blocks, slot_ref, tail_ref, used_ref, src_ref, o_hbm,
                     zeros, sem, zsem):
    i = pl.program_id(0)

    def zero_block(b):
        return pltpu.make_async_copy(zeros, o_hbm.at[pl.ds(pl.multiple_of(b * tm, tm), tm)], zsem)

    @pl.when(i == 0)
    def _():
        zeros[...] = jnp.zeros_like(zeros)
        used = used_ref[0]

        def start_tail(e, c):
            zero_block(tail_ref[e]).start()
            return c

        def start_unused(b, c):
            zero_block(b).start()
            return c

        def wait_one(b, c):
            zero_block(0).wait()
            return c

        lax.fori_loop(0, n_exp, start_tail, 0)
        lax.fori_loop(used, n_blocks, start_unused, 0)
        lax.fori_loop(0, n_exp + n_blocks - used, wait_one, 0)

    def issue(r, c):
        tok = i * tb + r
        for k in range(TOP_K):
            pltpu.make_async_copy(src_ref.at[pl.ds(r, 1)],
                                  o_hbm.at[pl.ds(slot_ref[tok * TOP_K + k], 1)], sem).start()
        return c

    lax.fori_loop(0, tb, issue, 0)
    for _ in range(TOP_K):
        pltpu.make_async_copy(src_ref, o_hbm.at[pl.ds(0, tb)], sem).wait()


def dispatch_rows(src, slots_flat, tail_blk, used_blocks, n_slots, tm, tb=128):
    t, w = src.shape
    n_exp = tail_blk.shape[0]
    n_blocks = n_slots // tm
    grid_spec = pltpu.PrefetchScalarGridSpec(
        num_scalar_prefetch=3,
        grid=(t // tb,),
        in_specs=[pl.BlockSpec((tb, w), lambda i, *_: (i, 0))],
        out_specs=pl.BlockSpec(memory_space=pl.ANY),
        scratch_shapes=[pltpu.VMEM((tm, w), src.dtype), pltpu.SemaphoreType.DMA(()),
                        pltpu.SemaphoreType.DMA(())])
    return pl.pallas_call(
        functools.partial(_dispatch_kernel, tb, tm, n_exp, n_blocks),
        grid_spec=grid_spec,
        out_shape=jax.ShapeDtypeStruct((n_slots, w), src.dtype),
        compiler_params=_cp("arbitrary"),
        name="dispatch_rows",
    )(slots_flat, tail_blk, used_blocks, src)


def _stream_group_weights(s, sf_ref, sg_ref, ng_ref, copies, consume):
    @pl.when(sf_ref[s] == 1)
    def _():
        g = sg_ref[s]

        @pl.when(g == 0)
        def _():
            for c in copies(0):
                c.start()

        for c in copies(g):
            c.wait()
        consume()

        @pl.when(g + 1 < ng_ref[0])
        def _():
            for c in copies(g + 1):
                c.start()


def _gate_up_kernel(layer, half, tn, sb_ref, so_ref, sf_ref, sg_ref, nv_ref, ge_ref, gj_ref, ng_ref,
                    x_ref, wg_hbm, wu_hbm, o_ref, stage, wg_s, wu_s, sem):
    s = pl.program_id(0)

    @pl.when(s >= nv_ref[0])
    def _():
        o_ref[...] = jnp.zeros_like(o_ref)

    def copies(g):
        col = pl.ds(pl.multiple_of(gj_ref[g] * tn, tn), tn)
        return (pltpu.make_async_copy(wg_hbm.at[layer, ge_ref[g], :, col], stage.at[0], sem.at[0]),
                pltpu.make_async_copy(wu_hbm.at[layer, ge_ref[g], :, col], stage.at[1], sem.at[0]))

    def consume():
        wg_s[...] = stage[0].astype(BF16)
        wu_s[...] = stage[1].astype(BF16)

    _stream_group_weights(s, sf_ref, sg_ref, ng_ref, copies, consume)

    @pl.when(s < nv_ref[0])
    def _():
        p = x_ref[...]
        lo = _unpack_lo(p).astype(BF16)
        hi = _unpack_hi(p).astype(BF16)
        gt = _dot(lo, wg_s[:half]) + _dot(hi, wg_s[half:])
        up = _dot(lo, wu_s[:half]) + _dot(hi, wu_s[half:])
        o_ref[...] = (_silu(gt) * up).astype(BF16)


def gate_up(x_packed, w_gate, w_up, layer, steps, tm, tn):
    n_slots, half = x_packed.shape
    _, _, d, f = w_gate.shape
    n_steps = steps[0].shape[0]
    grid_spec = pltpu.PrefetchScalarGridSpec(
        num_scalar_prefetch=len(steps),
        grid=(n_steps,),
        in_specs=[pl.BlockSpec((tm, half), lambda s, sb, *_: (sb[s], 0)),
                  pl.BlockSpec(memory_space=pl.ANY), pl.BlockSpec(memory_space=pl.ANY)],
        out_specs=pl.BlockSpec((tm, tn), lambda s, sb, so, *_: (sb[s], so[s])),
        scratch_shapes=[pltpu.VMEM((2, d, tn), F32), pltpu.VMEM((d, tn), BF16),
                        pltpu.VMEM((d, tn), BF16), pltpu.SemaphoreType.DMA((1,))])
    return pl.pallas_call(
        functools.partial(_gate_up_kernel, layer, half, tn),
        grid_spec=grid_spec,
        out_shape=jax.ShapeDtypeStruct((n_slots, f), BF16),
        compiler_params=_cp("arbitrary"),
        name="gate_up",
    )(*steps, x_packed, w_gate, w_up)


def _down_kernel(layer, tn, sb_ref, so_ref, sf_ref, sg_ref, nv_ref, ge_ref, gj_ref, ng_ref,
                 h_ref, w_hbm, o_ref, stage, w_s, sem):
    s = pl.program_id(0)

    @pl.when(s >= nv_ref[0])
    def _():
        o_ref[...] = jnp.zeros_like(o_ref)

    def copies(g):
        col = pl.ds(pl.multiple_of(gj_ref[g] * tn, tn), tn)
        return (pltpu.make_async_copy(w_hbm.at[layer, ge_ref[g], :, col], stage, sem.at[0]),)

    def consume():
        w_s[...] = stage[...].astype(BF16)

    _stream_group_weights(s, sf_ref, sg_ref, ng_ref, copies, consume)

    @pl.when(s < nv_ref[0])
    def _():
        y = _dot(h_ref[...], w_s[...])
        hw = y.shape[1] // 2
        o_ref[...] = _pack_words(y[:, :hw], y[:, hw:])


def down(h1, w_down, layer, steps, tm, tn):
    n_slots, f = h1.shape
    d = w_down.shape[3]
    n_steps = steps[0].shape[0]
    grid_spec = pltpu.PrefetchScalarGridSpec(
        num_scalar_prefetch=len(steps),
        grid=(n_steps,),
        in_specs=[pl.BlockSpec((tm, f), lambda s, sb, *_: (sb[s], 0)),
                  pl.BlockSpec(memory_space=pl.ANY)],
        out_specs=pl.BlockSpec((tm, tn // 2), lambda s, sb, so, *_: (sb[s], so[s])),
        scratch_shapes=[pltpu.VMEM((f, tn), F32), pltpu.VMEM((f, tn), BF16),
                        pltpu.SemaphoreType.DMA((1,))])
    return pl.pallas_call(
        functools.partial(_down_kernel, layer, tn),
        grid_spec=grid_spec,
        out_shape=jax.ShapeDtypeStruct((n_slots, d // 2), U32),
        compiler_params=_cp("arbitrary"),
        name="down",
    )(*steps, h1, w_down)


def _combine_kernel(npb, tb, tn, slot_ref, y_hbm, w_ref, sh_ref, x_ref, gp_ref, gs_ref, o_ref,
                    buf, sem):
    i = pl.program_id(0)
    n = pl.num_programs(0)

    def start_block(blk, slot):
        def body(r, c):
            tok = blk * tb + r
            for k in range(TOP_K):
                pltpu.make_async_copy(y_hbm.at[pl.ds(slot_ref[tok * TOP_K + k], 1)],
                                      buf.at[slot, pl.ds(k * tb + r, 1)], sem.at[slot]).start()
            return c
        lax.fori_loop(0, tb, body, 0)

    @pl.when(i == 0)
    def _():
        start_block(0, 0)

    @pl.when(i + 1 < n)
    def _():
        start_block(i + 1, (i + 1) % 2)

    cur = i % 2
    pltpu.make_async_copy(y_hbm.at[pl.ds(0, TOP_K * tb)], buf.at[cur], sem.at[cur]).wait()

    smp = i >= npb
    gate = lambda c0, c1: jnp.where(smp, gs_ref[:, c0:c1], gp_ref[0, :, c0:c1])
    w = w_ref[...]
    sh = sh_ref
    hw = tn // 2
    cw = min(hw, COMBINE_CHUNK)
    for c0 in range(0, sh_ref.shape[1], cw):
        a = (c0 // hw) * tn + c0 % hw
        lo = _unpack_lo(sh[:, c0:c0 + cw])
        hi = _unpack_hi(sh[:, c0:c0 + cw])
        for k in range(TOP_K):
            p = buf[cur, k * tb:(k + 1) * tb, c0:c0 + cw]
            wk = w[:, k:k + 1]
            lo = lo + wk * _unpack_lo(p)
            hi = hi + wk * _unpack_hi(p)
        o_ref[:, a:a + cw] = x_ref[:, a:a + cw] + gate(a, a + cw) * lo
        o_ref[:, a + hw:a + hw + cw] = (x_ref[:, a + hw:a + hw + cw]
                                        + gate(a + hw, a + hw + cw) * hi)


def combine(y_sorted, slots_flat, wts, y_shared, x, mods, k_gate, tn, tb=64):
    t, d = x.shape
    half = d // 2
    row = lambda i, sl: i
    grid_spec = pltpu.PrefetchScalarGridSpec(
        num_scalar_prefetch=1,
        grid=(t // tb,),
        in_specs=[pl.BlockSpec(memory_space=pl.ANY),
                  pl.BlockSpec((tb, TOP_K), lambda i, sl: (i, 0)),
                  pl.BlockSpec((tb, half), lambda i, sl: (i, 0)),
                  pl.BlockSpec((tb, d), lambda i, sl: (i, 0)),
                  *mods.specs(k_gate, tb, row)],
        out_specs=pl.BlockSpec((tb, d), lambda i, sl: (i, 0)),
        scratch_shapes=[pltpu.VMEM((2, TOP_K * tb, half), U32), pltpu.SemaphoreType.DMA((2,))])
    return pl.pallas_call(
        functools.partial(_combine_kernel, mods.tp // tb, tb, tn),
        grid_spec=grid_spec,
        out_shape=jax.ShapeDtypeStruct((t, d), F32),
        compiler_params=_cp("arbitrary"),
        name="combine",
    )(slots_flat, y_sorted, wts, y_shared, x, *mods.args())


def _dispatch_tables(idx, n_exp, tm, n_tiles):
    t, k = idx.shape
    a = t * k
    flat_e = idx.reshape(a)
    onehot = flat_e[:, None] == jnp.arange(n_exp, dtype=I32)[None, :]
    csum = jnp.cumsum(onehot.astype(I32), axis=0)
    counts = csum[-1]
    nblk = (counts + tm - 1) // tm
    blk_start = jnp.cumsum(nblk) - nblk
    slot = jnp.sum(jnp.where(onehot, (blk_start * tm)[None, :] + csum - 1, 0), axis=1)
    n_blocks = a // tm + n_exp
    n_slots = n_blocks * tm
    tail_blk = (blk_start + jnp.maximum(nblk - 1, 0)).astype(I32)
    used = jnp.sum(nblk).reshape(1).astype(I32)
    steps = [_grouped_steps(nblk, blk_start, n_blocks, nt) for nt in n_tiles]
    return slot.astype(I32), tail_blk, used, steps, n_slots


def _lookup(table, idx):
    n = table.shape[0]
    hit = idx[:, None] == jnp.arange(n, dtype=I32)[None, :]
    return jnp.sum(jnp.where(hit, table[None, :], 0), axis=1)


def _grouped_steps(nblk, blk_start, n_blocks, n_tiles):
    n_exp = nblk.shape[0]
    n_steps = n_blocks * n_tiles
    step_end = jnp.cumsum(nblk) * n_tiles
    total = step_end[-1]
    sidx = jnp.arange(n_steps, dtype=I32)
    valid = sidx < total
    s = jnp.minimum(sidx, total - 1)
    e = jnp.sum(step_end[None, :] <= s[:, None], axis=1).astype(I32)
    nb_e = _lookup(nblk, e)
    r = s - (_lookup(step_end, e) - nb_e * n_tiles)
    j = r // nb_e
    i = r % nb_e
    spare = sidx - total
    blk = jnp.where(valid, _lookup(blk_start, e) + i, jnp.sum(nblk) + spare // n_tiles)
    j_out = jnp.where(valid, j, spare % n_tiles)
    first = jnp.where((i == 0) & valid, 1, 0)
    used = (nblk > 0).astype(I32)
    used_before = jnp.cumsum(used) - used
    grp = _lookup(used_before, e) * n_tiles + j
    n_grp = jnp.sum(used) * n_tiles
    gidx = jnp.arange(n_exp * n_tiles, dtype=I32)
    g_e = jnp.minimum(jnp.sum((used_before + used)[None, :] <= (gidx // n_tiles)[:, None], axis=1),
                      n_exp - 1)
    as_i32 = lambda a: a.astype(I32)
    return (as_i32(blk), as_i32(j_out), as_i32(first), as_i32(grp), as_i32(total.reshape(1)),
            as_i32(g_e), as_i32(gidx % n_tiles), as_i32(n_grp.reshape(1)))


def _dense_steps(n_blocks, n_tiles):
    s = jnp.arange(n_blocks * n_tiles, dtype=I32)
    j = s // n_blocks
    i = s % n_blocks
    g = jnp.arange(n_tiles, dtype=I32)
    return (i, j, (i == 0).astype(I32), j, jnp.full((1,), n_blocks * n_tiles, I32),
            jnp.zeros_like(g), g, jnp.full((1,), n_tiles, I32))


def moe_ffn(x, g, mods, layer, router_wt, router_bias, w_gate, w_up, w_down, sw_gate, sw_up,
            sw_down, tl):
    t, d = x.shape
    _, n_exp, _, f = w_gate.shape
    tm, tn_up, tn_down = tl["moe_tm"], tl["tn_up"], tl["tn_down"]
    hp, idx_t, wts_t = ffn_in(x, g, mods, router_wt, router_bias, tm=tl["ffn_tm"])
    idx = idx_t.T
    wts = wts_t.T
    n_up, n_dn = f // tn_up, d // tn_down
    slot, tail_blk, used, (steps_up, steps_dn), n_slots = _dispatch_tables(idx, n_exp, tm,
                                                                           (n_up, n_dn))
    xs = dispatch_rows(hp, slot, tail_blk, used, n_slots, tm, tb=tl["dispatch_tb"])
    h1 = gate_up(xs, w_gate, w_up, layer, steps_up, tm, tn_up)
    ys = down(h1, w_down, layer, steps_dn, tm, tn_down)
    sh_steps_up = _dense_steps(t // tm, sw_gate.shape[3] // tn_up)
    sh_steps_dn = _dense_steps(t // tm, d // tn_down)
    s1 = gate_up(hp, sw_gate, sw_up, layer, sh_steps_up, tm, tn_up)
    ysh = down(s1, sw_down, layer, sh_steps_dn, tm, tn_down)
    return combine(ys, slot, wts, ysh, x, mods, 5, tn_down, tb=tl["comb_tb"])


def _rope_tables(pos):
    half = QK_ROPE // 2
    inv_freq = ROPE_THETA ** (-jnp.arange(half, dtype=F32) / half)
    ang = pos.astype(F32)[:, None] * inv_freq[None, :]
    z = jnp.zeros((pos.shape[0], LANES - QK_ROPE), F32)
    cos = jnp.concatenate([jnp.cos(ang), jnp.cos(ang), z], axis=1)
    sin = jnp.concatenate([jnp.sin(ang), jnp.sin(ang), z], axis=1)
    return cos, sin


def _rot_cols(w):
    half = w.shape[-1] // 2
    return jnp.concatenate([-w[..., half:], w[..., :half]], axis=-1)


def _pad_cols(w, n):
    return jnp.concatenate([w, jnp.zeros(w.shape[:-1] + (n - w.shape[-1],), w.dtype)], axis=-1)


TILES = dict(adaln_tn=512, norm_tm=256, mla_tm=256, qup_tm=512, kv_tm=512, kv_tn=1024,
             proj_tm=512, proj_tn=1024, tq=1024, tk=1024, n_pg=32, pool_ts=256, pool_seqs=8,
             ffn_tm=256, dispatch_tb=128, moe_tm=256, tn_up=512, tn_down=4096, comb_tb=128)


def kernel(x_prompt, x_sample, cache_kv_latent, cache_k_rope, state_pool, page_table, c_prompt,
           c_sample, ada_w, ada_b, norm_mix_g, norm_ffn_g, mla_w_in, mla_q_norm_g, mla_w_uq,
           mla_kv_norm_g, mla_w_uk, mla_w_uv, mla_w_o, pool_w, pool_scale, router_w, router_bias,
           exp_w_gate, exp_w_up, exp_w_down, shared_w_gate, shared_w_up, shared_w_down,
           final_norm_g):
    return _step(TILES, x_prompt, x_sample, cache_kv_latent, cache_k_rope, state_pool, page_table,
                 c_prompt, c_sample, ada_w, ada_b, norm_mix_g, norm_ffn_g, mla_w_in, mla_q_norm_g,
                 mla_w_uq, mla_kv_norm_g, mla_w_uk, mla_w_uv, mla_w_o, pool_w, pool_scale,
                 router_w, router_bias, exp_w_gate, exp_w_up, exp_w_down, shared_w_gate,
                 shared_w_up, shared_w_down, final_norm_g)


def _step(tl, x_prompt, x_sample, cache_kv_latent, cache_k_rope, state_pool, page_table, c_prompt,
          c_sample, ada_w, ada_b, norm_mix_g, norm_ffn_g, mla_w_in, mla_q_norm_g, mla_w_uq,
          mla_kv_norm_g, mla_w_uk, mla_w_uv, mla_w_o, pool_w, pool_scale, router_w, router_bias,
          exp_w_gate, exp_w_up, exp_w_down, shared_w_gate, shared_w_up, shared_w_down,
          final_norm_g):
    n_p, seq, d = x_prompt.shape
    n_s, dec_seq, _ = x_sample.shape
    depth = ada_w.shape[0]
    tp, ts_tok = n_p * seq, n_s * dec_seq
    t = tp + ts_tok
    page = cache_kv_latent.shape[2]
    past = page_table.shape[1] * page
    q_lora = mla_q_norm_g.shape[1]
    kv_lora = mla_kv_norm_g.shape[1]
    n_h = N_HEADS
    sm_scale = (QK_NOPE + QK_ROPE) ** -0.5
    n_pool_hist = max(POOL_WINDOWS) - 1

    x = None
    x_parts = (x_prompt.reshape(tp, d), x_sample.reshape(ts_tok, d))

    n_p_pad = -(-n_p // 8) * 8
    c_all = jnp.concatenate([c_prompt, jnp.zeros((n_p_pad - n_p, d), F32),
                             jnp.repeat(c_sample, dec_seq, axis=0)], axis=0)
    mod_all = adaln(c_all, ada_w, ada_b, tl["adaln_tn"])

    pos = jnp.concatenate([jnp.tile(jnp.arange(seq), n_p),
                           jnp.tile(past + jnp.arange(dec_seq), n_s)])
    cos_t, sin_t = _rope_tables(pos)

    lat_p, rope_p, pool_p, lat_s, rope_s, pool_s = [], [], [], [], [], []
    for i in range(depth):
        j = i // 2
        mods = Mods(mod_all[i, :n_p].reshape(n_p, 1, 6 * d), mod_all[i, n_p_pad:], seq, tp, d)
        if i % 2 == 0:
            w_in = mla_w_in[j]
            w_r = w_in[:, q_lora + kv_lora:]
            w_aug = jnp.concatenate([w_in[:, :q_lora + kv_lora], _pad_cols(w_r, LANES),
                                     _pad_cols(_rot_cols(w_r), LANES)], axis=1).astype(BF16)
            if x is not None:
                x_parts = (x[:tp], x[tp:])
            cq, ckv, ckv_b, kr, kr_b = mla_in(x_parts, norm_mix_g[i], mods, w_aug, mla_q_norm_g[j],
                                              mla_kv_norm_g[j], cos_t, sin_t, q_lora, kv_lora,
                                              tm=tl["mla_tm"])
            wq = mla_w_uq[j].reshape(q_lora, n_h, QK_NOPE + QK_ROPE)
            wq_r = wq[:, :, QK_NOPE:]
            w1 = jnp.concatenate([wq, jnp.zeros((q_lora, n_h, HEAD_PAD - QK_NOPE - QK_ROPE), F32)],
                                 axis=2).reshape(q_lora, n_h * HEAD_PAD).astype(BF16)
            w2 = _pad_cols(_rot_cols(wq_r), LANES).reshape(q_lora, n_h * LANES).astype(BF16)
            q_cat = q_up(cq, w1, w2, cos_t, sin_t, tm=tl["qup_tm"])

            w_kv = jnp.concatenate([mla_w_uk[j], mla_w_uv[j]], axis=1).astype(BF16)
            kv = matmul(ckv_b[:tp], w_kv, BF16, tl["kv_tm"], tl["kv_tn"])
            attn_p = prompt_attention(q_cat, kv, kr_b, n_p, seq, sm_scale, tl["tq"], tl["tk"])

            w_uk_t = mla_w_uk[j].reshape(kv_lora, n_h, QK_NOPE).transpose(1, 2, 0).astype(BF16)
            w_uv_h = mla_w_uv[j].reshape(kv_lora, n_h, V_HEAD).transpose(1, 0, 2).astype(BF16)
            q_s = q_cat[tp:].reshape(ts_tok, n_h, HEAD_PAD)
            q_nope_s = q_s[:, :, :QK_NOPE].transpose(1, 0, 2)
            q_lat = heads_matmul(
                q_nope_s, w_uk_t,
                pl.BlockSpec((1, ts_tok, QK_NOPE), lambda h: (h, 0, 0)),
                pl.BlockSpec((1, ts_tok, kv_lora), lambda h: (h, 0, 0)),
                jax.ShapeDtypeStruct((n_h, ts_tok, kv_lora), BF16))
            q_lat = q_lat.transpose(1, 0, 2).reshape(n_s, dec_seq * n_h, kv_lora)
            q_rope_s = q_s[:, :, QK_NOPE:QK_NOPE + QK_ROPE].reshape(n_s, dec_seq * n_h, QK_ROPE)
            kpad = page - dec_seq
            new_lat = jnp.pad(ckv_b[tp:].reshape(n_s, dec_seq, kv_lora), ((0, 0), (0, kpad), (0, 0)))
            new_rope = jnp.pad(kr_b[tp:, :QK_ROPE].reshape(n_s, dec_seq, QK_ROPE),
                               ((0, 0), (0, kpad), (0, 0))).swapaxes(1, 2)
            o_lat = sample_attention(q_lat, q_rope_s, cache_kv_latent,
                                     cache_k_rope.swapaxes(2, 3), j,
                                     page_table, new_lat, new_rope, dec_seq, sm_scale,
                                     n_pg=tl["n_pg"])
            o_lat = o_lat.reshape(ts_tok, n_h, kv_lora).transpose(1, 0, 2)
            attn_s = heads_matmul(
                o_lat, w_uv_h,
                pl.BlockSpec((1, ts_tok, kv_lora), lambda h: (h, 0, 0)),
                pl.BlockSpec((ts_tok, V_HEAD), lambda h: (0, h)),
                jax.ShapeDtypeStruct((ts_tok, n_h * V_HEAD), BF16))
            x = proj_residual((attn_p, attn_s), mla_w_o[j].astype(BF16), x_parts, mods, 2,
                              tl["proj_tm"], tl["proj_tn"])

            lat_p.append(ckv[:tp].reshape(n_p, seq, kv_lora))
            rope_p.append(kr[:tp, :QK_ROPE].reshape(n_p, seq, QK_ROPE))
            lat_s.append(ckv[tp:].reshape(n_s, dec_seq, kv_lora))
            rope_s.append(kr[tp:, :QK_ROPE].reshape(n_s, dec_seq, QK_ROPE))
        else:
            if x is None:
                x = jnp.concatenate(x_parts, axis=0)
            h = modnorm(x, norm_mix_g[i], mods, 0, 1, F32, tm=tl["norm_tm"])
            pw = pool_w[j].astype(BF16)
            gate_p = mods.mp[:, :, 2 * d:3 * d]
            xs3 = x[tp:].reshape(n_s, dec_seq, d)
            x_new = pool_mixer(h, x, gate_p, pw, pool_scale[j], n_p, seq, 0, tl["pool_ts"],
                               in_place=True)
            hs3 = h[tp:].reshape(n_s, dec_seq, d)
            rows = POOL_HALO + 8
            tail = rows - POOL_HALO - dec_seq
            lead = POOL_HALO - n_pool_hist
            hh = jnp.concatenate([jnp.zeros((n_s, lead, d), F32), state_pool[j], hs3,
                                  jnp.zeros((n_s, tail, d), F32)], axis=1)
            pad3 = lambda a: jnp.pad(a, ((0, 0), (POOL_HALO, tail), (0, 0)))
            gate_s = mods.ms[:, 2 * d:3 * d].reshape(n_s, dec_seq, d)
            seqs = tl["pool_seqs"]
            xs_new = pool_mixer(hh.reshape(n_s * rows, d), pad3(xs3).reshape(n_s * rows, d),
                                pad3(gate_s).reshape(n_s * rows, d), pw, pool_scale[j],
                                n_s // seqs, seqs * rows, past - POOL_HALO, seqs * rows, period=rows)
            xs_new = xs_new.reshape(n_s, rows, d)[:, POOL_HALO:POOL_HALO + dec_seq]
            x = lax.dynamic_update_slice(x_new, xs_new.reshape(ts_tok, d), (tp, 0))
            pool_p.append(jnp.stack([h[(b + 1) * seq - n_pool_hist:(b + 1) * seq]
                                     for b in range(n_p)]))
            pool_s.append(jnp.concatenate([state_pool[j], hs3], axis=1)[:, dec_seq:])

        x = moe_ffn(x, norm_ffn_g[i], mods, i, router_w[i].T, router_bias[i], exp_w_gate,
                    exp_w_up, exp_w_down, shared_w_gate[:, None], shared_w_up[:, None],
                    shared_w_down[:, None], tl)

    y_p, y_s = rmsnorm_split(x, final_norm_g, tp, tm=tl["norm_tm"])
    return (y_p.reshape(n_p, seq, d), y_s.reshape(n_s, dec_seq, d),
            jnp.stack(lat_p), jnp.stack(rope_p), jnp.stack(pool_p),
            jnp.stack(lat_s), jnp.stack(rope_s), jnp.stack(pool_s))
```
